```python
import math, functools
import jax, jax.numpy as jnp
from jax import lax
import numpy as np

D_MODEL = 1024
BATCH = 4
SEQ = 8192
DEPTH = 2
DEC_BATCH = 32
DEC_SEQ = 8
PAST_LEN = 16384
PAGE_SIZE = 128

N_A_LAYERS = DEPTH // 2
N_B_LAYERS = DEPTH - N_A_LAYERS
D_FF = 2816
SSM_GROUP = 16
N_GROUPS = D_MODEL // SSM_GROUP
SSM_STATE = 64
N_HEADS = 4
HEAD_DIM = D_MODEL // (2 * N_HEADS)
V_DIM = 2 * HEAD_DIM
Q_BLOCK = 128
EPS = 1e-6
SUBLN_EPS = 1e-5
DT_MIN = 1e-3
DT_MAX = 1e-1
NEG = -1e30

kernel_name = "yoco_s5_diff_attn_step"


def rmsnorm(x, g, eps=EPS):
    xf = x.astype(jnp.float32)
    y = xf * lax.rsqrt(jnp.mean(xf * xf, axis=-1, keepdims=True) + eps) * g.astype(jnp.float32)
    return y.astype(x.dtype)


def swiglu(h, w_gate, w_up, w_down):
    return (jax.nn.silu(h @ w_gate) * (h @ w_up)) @ w_down


def s5_discretise(a_re, a_im, log_dt, b_re, b_im):
    a_re = a_re.astype(jnp.float32)
    a_im = a_im.astype(jnp.float32)
    dt = jnp.exp(log_dt.astype(jnp.float32))[:, None]
    mag = jnp.exp(a_re * dt)
    lb_re = mag * jnp.cos(a_im * dt)
    lb_im = mag * jnp.sin(a_im * dt)
    num_re = lb_re - 1.0
    num_im = lb_im
    den = a_re * a_re + a_im * a_im
    f_re = (num_re * a_re + num_im * a_im) / den
    f_im = (num_im * a_re - num_re * a_im) / den
    b_re = b_re.astype(jnp.float32)
    b_im = b_im.astype(jnp.float32)
    bb_re = f_re[..., None] * b_re - f_im[..., None] * b_im
    bb_im = f_re[..., None] * b_im + f_im[..., None] * b_re
    return lb_re, lb_im, bb_re, bb_im


def _complex_affine_combine(e1, e2):
    a1r, a1i, b1r, b1i = e1
    a2r, a2i, b2r, b2i = e2
    return (a2r * a1r - a2i * a1i,
            a2r * a1i + a2i * a1r,
            a2r * b1r - a2i * b1i + b2r,
            a2r * b1i + a2i * b1r + b2i)


def s5_mixer(u, h0_re, h0_im, a_re, a_im, log_dt, b_re, b_im, c_re, c_im, d_skip, w_glu_a, w_glu_b):
    bsz, L, _ = u.shape
    uf = u.astype(jnp.float32)
    ug = uf.reshape(bsz, L, N_GROUPS, SSM_GROUP)
    lb_re, lb_im, bb_re, bb_im = s5_discretise(a_re, a_im, log_dt, b_re, b_im)
    bu_re = jnp.einsum("blgc,gnc->lbgn", ug, bb_re)
    bu_im = jnp.einsum("blgc,gnc->lbgn", ug, bb_im)
    if h0_re is not None:
        h_re = h0_re.astype(jnp.float32)
        h_im = h0_im.astype(jnp.float32)
        bu_re = bu_re.at[0].add(lb_re * h_re - lb_im * h_im)
        bu_im = bu_im.at[0].add(lb_re * h_im + lb_im * h_re)
    a_seq_re = jnp.broadcast_to(lb_re, (L, 1, N_GROUPS, SSM_STATE))
    a_seq_im = jnp.broadcast_to(lb_im, (L, 1, N_GROUPS, SSM_STATE))
    _, _, xs_re, xs_im = lax.associative_scan(
        _complex_affine_combine, (a_seq_re, a_seq_im, bu_re, bu_im), axis=0)
    y = (jnp.einsum("gcn,lbgn->blgc", c_re.astype(jnp.float32), xs_re)
         - jnp.einsum("gcn,lbgn->blgc", c_im.astype(jnp.float32), xs_im))
    y = y.reshape(bsz, L, D_MODEL) + d_skip.astype(jnp.float32) * uf
    g = jax.nn.gelu(y.astype(u.dtype))
    out = (g @ w_glu_a) * jax.nn.sigmoid(g @ w_glu_b)
    return out, xs_re[-1], xs_im[-1]


def _diff_combine(s, lam):
    p = jax.nn.softmax(s, axis=-1)
    b, _, t, S = p.shape
    p = p.reshape(b, N_HEADS, 2, t, S)
    return p[:, :, 0] - lam * p[:, :, 1]


def attend_prompt(q, k, v, lam):
    b, L = q.shape[:2]
    nb = L // Q_BLOCK
    scale = HEAD_DIM ** -0.5
    qb = q.reshape(b, nb, Q_BLOCK, 2 * N_HEADS, HEAD_DIM).transpose(1, 0, 2, 3, 4)
    k_pos = jnp.arange(L)

    def one_block(args):
        i, q_blk = args
        q_pos = i * Q_BLOCK + jnp.arange(Q_BLOCK)
        s = jnp.einsum("bthd,bshd->bhts", q_blk, k).astype(jnp.float32) * scale
        s = jnp.where(k_pos[None, :] <= q_pos[:, None], s, NEG)
        a = _diff_combine(s, lam)
        return jnp.einsum("bhts,bshe->bthe", a.astype(v.dtype), v)

    out = lax.map(one_block, (jnp.arange(nb), qb))
    return out.transpose(1, 0, 2, 3, 4).reshape(b, L, N_HEADS, V_DIM)


def attend_sample(q, k_new, v_new, lam, k_past, v_past):
    t = q.shape[1]
    P = k_past.shape[1]
    scale = HEAD_DIM ** -0.5
    s_past = jnp.einsum("bthd,bshd->bhts", q, k_past).astype(jnp.float32) * scale
    s_new = jnp.einsum("bthd,bshd->bhts", q, k_new).astype(jnp.float32) * scale
    causal = jnp.arange(t)[None, :] <= jnp.arange(t)[:, None]
    s_new = jnp.where(causal, s_new, NEG)
    a = _diff_combine(jnp.concatenate([s_past, s_new], axis=-1), lam).astype(v_new.dtype)
    return (jnp.einsum("bhts,bshe->bthe", a[..., :P], v_past)
            + jnp.einsum("bhts,bshe->bthe", a[..., P:], v_new))


def lambda_init_for(layer):
    return 0.8 - 0.6 * math.exp(-0.3 * layer)


def diff_attn_mixer(h, k, v, attend, w_q, lam_p, subln, w_o, lam_init):
    b, t, _ = h.shape
    q = (h @ w_q).reshape(b, t, 2 * N_HEADS, HEAD_DIM)
    lp = lam_p.astype(jnp.float32)
    lam = jnp.exp(jnp.sum(lp[0] * lp[1])) - jnp.exp(jnp.sum(lp[2] * lp[3])) + lam_init
    o = attend(q, k, v, lam)
    o = rmsnorm(o, subln, SUBLN_EPS) * (1.0 - lam_init)
    return o.reshape(b, t, D_MODEL) @ w_o


def trunk(x, h0_re, h0_im, attend, norm_gains, ffn_w_gate, ffn_w_up, ffn_w_down,
          ssm_a_re, ssm_a_im, ssm_log_dt, ssm_b_re, ssm_b_im, ssm_c_re, ssm_c_im, ssm_d,
          ssm_w_glu_a, ssm_w_glu_b, kv_norm, w_k, w_v, attn_w_q, attn_lambda, attn_subln, attn_w_o):
    bsz, t, _ = x.shape
    new_re, new_im = [], []
    k_sh, v_sh = None, None
    for layer in range(DEPTH):
        g = norm_gains[layer]
        x = x + 0.5 * rmsnorm(swiglu(rmsnorm(x, g[0]), ffn_w_gate[layer, 0], ffn_w_up[layer, 0],
                                     ffn_w_down[layer, 0]), g[1])
        h = rmsnorm(x, g[2])
        if layer < N_A_LAYERS:
            mixed, hr, hi = s5_mixer(
                h, None if h0_re is None else h0_re[layer], None if h0_im is None else h0_im[layer],
                ssm_a_re[layer], ssm_a_im[layer], ssm_log_dt[layer], ssm_b_re[layer], ssm_b_im[layer],
                ssm_c_re[layer], ssm_c_im[layer], ssm_d[layer], ssm_w_glu_a[layer], ssm_w_glu_b[layer])
            new_re.append(hr)
            new_im.append(hi)
        else:
            j = layer - N_A_LAYERS
            mixed = diff_attn_mixer(h, k_sh, v_sh, attend, attn_w_q[j], attn_lambda[j],
                                    attn_subln[j], attn_w_o[j], lambda_init_for(layer))
        x = x + rmsnorm(mixed, g[3])
        x = x + 0.5 * rmsnorm(swiglu(rmsnorm(x, g[4]), ffn_w_gate[layer, 1], ffn_w_up[layer, 1],
                                     ffn_w_down[layer, 1]), g[5])
        if layer == N_A_LAYERS - 1:
            kv_in = rmsnorm(x, kv_norm)
            k_sh = (kv_in @ w_k).reshape(bsz, t, 2 * N_HEADS, HEAD_DIM)
            v_sh = (kv_in @ w_v).reshape(bsz, t, N_HEADS, V_DIM)
    return x, k_sh, v_sh, jnp.stack(new_re, axis=0), jnp.stack(new_im, axis=0)


def setup_inputs(seed: int = 0) -> dict:
    key = jax.random.key(seed)
    ks = jax.random.split(key, 32)
    f32 = jnp.float32
    n_pages = PAST_LEN // PAGE_SIZE
    n_pool = (DEC_BATCH * n_pages * 5) // 4
    d_in = D_MODEL ** -0.5

    def nrm(k, shape, s):
        return jax.random.normal(k, shape, f32) * s

    x_prompt = nrm(ks[0], (BATCH, SEQ, D_MODEL), 1.0)
    x_sample = nrm(ks[1], (DEC_BATCH, DEC_SEQ, D_MODEL), 1.0)
    cache_k = nrm(ks[2], (n_pool, PAGE_SIZE, 2 * N_HEADS, HEAD_DIM), 1.0)
    cache_v = nrm(ks[3], (n_pool, PAGE_SIZE, N_HEADS, V_DIM), 1.0)
    state_ssm_re = nrm(ks[4], (N_A_LAYERS, DEC_BATCH, N_GROUPS, SSM_STATE), 0.1)
    state_ssm_im = nrm(ks[5], (N_A_LAYERS, DEC_BATCH, N_GROUPS, SSM_STATE), 0.1)
    page_table = jax.random.permutation(ks[6], n_pool)[: DEC_BATCH * n_pages].reshape(
        DEC_BATCH, n_pages).astype(jnp.int32)

    norm_gains = 1.0 + nrm(ks[7], (DEPTH, 6, D_MODEL), 0.02)
    ffn_w_gate = nrm(ks[8], (DEPTH, 2, D_MODEL, D_FF), d_in)
    ffn_w_up = nrm(ks[9], (DEPTH, 2, D_MODEL, D_FF), d_in)
    ffn_w_down = nrm(ks[10], (DEPTH, 2, D_FF, D_MODEL), D_FF ** -0.5)

    n_idx = jnp.arange(SSM_STATE, dtype=f32)
    ssm_a_re = -0.5 + nrm(ks[11], (N_A_LAYERS, N_GROUPS, SSM_STATE), 0.01)
    ssm_a_im = jnp.pi * n_idx + nrm(ks[12], (N_A_LAYERS, N_GROUPS, SSM_STATE), 0.01)
    ssm_log_dt = jax.random.uniform(ks[13], (N_A_LAYERS, N_GROUPS), f32,
                                    minval=math.log(DT_MIN), maxval=math.log(DT_MAX))
    b_s = (2.0 * SSM_GROUP) ** -0.5
    ssm_b_re = nrm(ks[14], (N_A_LAYERS, N_GROUPS, SSM_STATE, SSM_GROUP), b_s)
    ssm_b_im = nrm(ks[15], (N_A_LAYERS, N_GROUPS, SSM_STATE, SSM_GROUP), b_s)
    c_s = (2.0 * SSM_STATE) ** -0.5
    ssm_c_re = nrm(ks[16], (N_A_LAYERS, N_GROUPS, SSM_GROUP, SSM_STATE), c_s)
    ssm_c_im = nrm(ks[17], (N_A_LAYERS, N_GROUPS, SSM_GROUP, SSM_STATE), c_s)
    ssm_d = nrm(ks[18], (N_A_LAYERS, D_MODEL), 1.0)
    ssm_w_glu_a = nrm(ks[19], (N_A_LAYERS, D_MODEL, D_MODEL), d_in)
    ssm_w_glu_b = nrm(ks[20], (N_A_LAYERS, D_MODEL, D_MODEL), d_in)

    kv_norm = 1.0 + nrm(ks[21], (D_MODEL,), 0.02)
    w_k = nrm(ks[22], (D_MODEL, 2 * N_HEADS * HEAD_DIM), d_in)
    w_v = nrm(ks[23], (D_MODEL, N_HEADS * V_DIM), d_in)
    attn_w_q = nrm(ks[24], (N_B_LAYERS, D_MODEL, 2 * N_HEADS * HEAD_DIM), d_in)
    attn_lambda = nrm(ks[25], (N_B_LAYERS, 4, HEAD_DIM), 0.1)
    attn_subln = 1.0 + nrm(ks[26], (N_B_LAYERS, V_DIM), 0.02)
    attn_w_o = nrm(ks[27], (N_B_LAYERS, D_MODEL, D_MODEL), d_in)

    return {"x_prompt": x_prompt, "x_sample": x_sample, "cache_k": cache_k, "cache_v": cache_v,
            "state_ssm_re": state_ssm_re, "state_ssm_im": state_ssm_im, "page_table": page_table,
            "norm_gains": norm_gains, "ffn_w_gate": ffn_w_gate, "ffn_w_up": ffn_w_up,
            "ffn_w_down": ffn_w_down, "ssm_a_re": ssm_a_re, "ssm_a_im": ssm_a_im,
            "ssm_log_dt": ssm_log_dt, "ssm_b_re": ssm_b_re, "ssm_b_im": ssm_b_im,
            "ssm_c_re": ssm_c_re, "ssm_c_im": ssm_c_im, "ssm_d": ssm_d,
            "ssm_w_glu_a": ssm_w_glu_a, "ssm_w_glu_b": ssm_w_glu_b, "kv_norm": kv_norm,
            "w_k": w_k, "w_v": w_v, "attn_w_q": attn_w_q, "attn_lambda": attn_lambda,
            "attn_subln": attn_subln, "attn_w_o": attn_w_o}


def reference(x_prompt, x_sample, cache_k, cache_v, state_ssm_re, state_ssm_im, page_table,
              norm_gains, ffn_w_gate, ffn_w_up, ffn_w_down, ssm_a_re, ssm_a_im, ssm_log_dt,
              ssm_b_re, ssm_b_im, ssm_c_re, ssm_c_im, ssm_d, ssm_w_glu_a, ssm_w_glu_b,
              kv_norm, w_k, w_v, attn_w_q, attn_lambda, attn_subln, attn_w_o):
    params = dict(norm_gains=norm_gains, ffn_w_gate=ffn_w_gate, ffn_w_up=ffn_w_up,
                  ffn_w_down=ffn_w_down, ssm_a_re=ssm_a_re, ssm_a_im=ssm_a_im,
                  ssm_log_dt=ssm_log_dt, ssm_b_re=ssm_b_re, ssm_b_im=ssm_b_im,
                  ssm_c_re=ssm_c_re, ssm_c_im=ssm_c_im, ssm_d=ssm_d,
                  ssm_w_glu_a=ssm_w_glu_a, ssm_w_glu_b=ssm_w_glu_b, kv_norm=kv_norm,
                  w_k=w_k, w_v=w_v, attn_w_q=attn_w_q, attn_lambda=attn_lambda,
                  attn_subln=attn_subln, attn_w_o=attn_w_o)

    y_prompt, k_prompt, v_prompt, ssm_re_prompt, ssm_im_prompt = trunk(
        x_prompt, None, None, attend_prompt, **params)

    dec_b, n_pages = page_table.shape
    past = n_pages * PAGE_SIZE
    k_past = cache_k[page_table].reshape(dec_b, past, 2 * N_HEADS, HEAD_DIM)
    v_past = cache_v[page_table].reshape(dec_b, past, N_HEADS, V_DIM)
    attend_s = functools.partial(attend_sample, k_past=k_past, v_past=v_past)
    y_sample, k_sample, v_sample, ssm_re_sample, ssm_im_sample = trunk(
        x_sample, state_ssm_re, state_ssm_im, attend_s, **params)

    return (y_prompt, y_sample, k_prompt, v_prompt, ssm_re_prompt, ssm_im_prompt,
            k_sample, v_sample, ssm_re_sample, ssm_im_sample)
```

```python
import functools
import math

import jax
import jax.numpy as jnp
from jax import lax
from jax.experimental import pallas as pl
from jax.experimental.pallas import tpu as pltpu

F32 = jnp.float32
BF16 = jnp.bfloat16

EPS = 1e-6
SUBLN_EPS = 1e-5
NEG = -1e30

LANES = 128
SUBLANES = 8
VMEM_LIMIT = 56 * 1024 * 1024

SSM_GROUP = 16
SSM_STATE = 64
GROUPS_PER_BLOCK = LANES // SSM_GROUP
CHUNK = 4
PAGES_PER_STEP = 8


def _cparams(sem):
    return pltpu.CompilerParams(dimension_semantics=sem, vmem_limit_bytes=VMEM_LIMIT)


def _const_spec(shape):
    nd = len(shape)
    return pl.BlockSpec(shape, lambda *_: (0,) * nd, pipeline_mode=pl.Buffered(1))


def _rms(x, g, eps):
    return x * lax.rsqrt(jnp.mean(x * x, axis=-1, keepdims=True) + eps) * g


def _dot(a, b):
    return jnp.dot(a, b, preferred_element_type=F32)


def _dot_nt(a, b):
    return lax.dot_general(a, b, (((1,), (1,)), ((), ())), preferred_element_type=F32)


def _ffn_body(*refs, has_o, has_kv, has_q, d_ff, f_chunk):
    it = iter(refs)
    x_ref = next(it)
    if has_o:
        o_ref, wo_ref, g3_ref = next(it), next(it), next(it)
    g0_ref, g1_ref, wg_ref, wu_ref, wd_ref = (next(it) for _ in range(5))
    if has_kv:
        kvn_ref, wk_ref, wv_ref = next(it), next(it), next(it)
    if has_q:
        g2_ref, wq_ref = next(it), next(it)
    y_ref = next(it)
    if has_kv:
        k32_ref, v32_ref, k16_ref, v16_ref = (next(it) for _ in range(4))
    if has_q:
        q_ref = next(it)

    x = x_ref[...]
    if has_o:
        x = x + _rms(_dot(o_ref[...], wo_ref[...]), g3_ref[...], EPS)
    h = _rms(x, g0_ref[...], EPS).astype(BF16)
    acc = None
    for c in range(0, d_ff, f_chunk):
        g = _dot(h, wg_ref[:, c:c + f_chunk])
        u = _dot(h, wu_ref[:, c:c + f_chunk])
        a = (g * jax.nn.sigmoid(g) * u).astype(BF16)
        d = _dot(a, wd_ref[c:c + f_chunk, :])
        acc = d if acc is None else acc + d
    y = x + 0.5 * _rms(acc, g1_ref[...], EPS)
    y_ref[...] = y
    if has_kv:
        kv_in = _rms(y, kvn_ref[...], EPS).astype(BF16)
        k = _dot(kv_in, wk_ref[...])
        v = _dot(kv_in, wv_ref[...])
        k32_ref[...] = k
        v32_ref[...] = v
        k16_ref[...] = k.astype(BF16)
        v16_ref[...] = v.astype(BF16)
    if has_q:
        hq = _rms(y, g2_ref[...], EPS).astype(BF16)
        q_ref[...] = _dot(hq, wq_ref[...]).astype(BF16)


def _ffn_call(x, g_pre, g_post, wg, wu, wd, *, o=None, wo=None, g_o=None,
              kvn=None, wk=None, wv=None, g_q=None, wq=None, tm):
    n, d = x.shape
    d_ff = wg.shape[1]
    has_o, has_kv, has_q = o is not None, wk is not None, wq is not None
    row = pl.BlockSpec((tm, d), lambda i: (i, 0))
    args, specs = [x], [row]
    if has_o:
        args += [o, wo, g_o]
        specs += [row, _const_spec(wo.shape), _const_spec(g_o.shape)]
    args += [g_pre, g_post, wg, wu, wd]
    specs += [_const_spec(a.shape) for a in (g_pre, g_post, wg, wu, wd)]
    out_shape = [jax.ShapeDtypeStruct((n, d), F32)]
    out_specs = [row]
    if has_kv:
        args += [kvn, wk, wv]
        specs += [_const_spec(a.shape) for a in (kvn, wk, wv)]
        out_shape += [jax.ShapeDtypeStruct((n, d), F32), jax.ShapeDtypeStruct((n, d), F32),
                      jax.ShapeDtypeStruct((n, d), BF16), jax.ShapeDtypeStruct((n, d), BF16)]
        out_specs += [row, row, row, row]
    if has_q:
        args += [g_q, wq]
        specs += [_const_spec(g_q.shape), _const_spec(wq.shape)]
        out_shape.append(jax.ShapeDtypeStruct((n, d), BF16))
        out_specs.append(row)
    f_chunk = d_ff // 2 if (d_ff // 2) % LANES == 0 else d_ff
    body = functools.partial(_ffn_body, has_o=has_o, has_kv=has_kv, has_q=has_q,
                             d_ff=d_ff, f_chunk=f_chunk)
    return pl.pallas_call(
        body, grid=(n // tm,), in_specs=specs, out_specs=out_specs, out_shape=out_shape,
        compiler_params=_cparams(("parallel",)), name="ffn")(*args)


def _s5_disc_body(are_ref, aim_ref, ldt_ref, lbr_ref, lbi_ref, fr_ref, fi_ref):
    a_re, a_im = are_ref[...], aim_ref[...]
    dt = jnp.exp(ldt_ref[...])
    mag = jnp.exp(a_re * dt)
    lb_re = mag * jnp.cos(a_im * dt)
    lb_im = mag * jnp.sin(a_im * dt)
    num_re = lb_re - 1.0
    den = a_re * a_re + a_im * a_im
    lbr_ref[...] = lb_re
    lbi_ref[...] = lb_im
    fr_ref[...] = (num_re * a_re + lb_im * a_im) / den
    fi_ref[...] = (lb_im * a_re - num_re * a_im) / den


def _s5_discretise(a_re, a_im, log_dt):
    g, n = a_re.shape
    out = jax.ShapeDtypeStruct((g, n), F32)
    return pl.pallas_call(_s5_disc_body, out_shape=[out] * 4, name="s5_disc")(
        a_re, a_im, jnp.broadcast_to(log_dt[:, None], (g, n)))


def _cmul(ar, ai, br, bi):
    return ar * br - ai * bi, ar * bi + ai * br


def _s5_operators(lb_re, lb_im, f_re, f_im, b_re, b_im, c_re, c_im, chunk):
    g, n = lb_re.shape
    nb, gl = g // GROUPS_PER_BLOCK, GROUPS_PER_BLOCK
    c = b_re.shape[-1]
    hi = lax.Precision.HIGHEST
    bb_re = f_re[..., None] * b_re - f_im[..., None] * b_im
    bb_im = f_re[..., None] * b_im + f_im[..., None] * b_re
    pw_re, pw_im = [jnp.ones_like(lb_re)], [jnp.zeros_like(lb_re)]
    for _ in range(8 * chunk):
        r, i = _cmul(pw_re[-1], pw_im[-1], lb_re, lb_im)
        pw_re.append(r)
        pw_im.append(i)
    eye = jnp.eye(gl, dtype=F32)

    taps = []
    for j in range(chunk):
        dr, di = _cmul(c_re, c_im, pw_re[j][:, None, :], pw_im[j][:, None, :])
        taps.append(jnp.einsum("gon,gni->gio", dr, bb_re, precision=hi)
                    - jnp.einsum("gon,gni->gio", di, bb_im, precision=hi))
    zero = jnp.zeros_like(taps[0])
    conv = jnp.stack([jnp.stack([taps[t - s] if t >= s else zero for t in range(chunk)], 0)
                      for s in range(chunk)], 0)
    conv = conv.reshape(chunk, chunk, nb, gl, c, c)
    m_op = jnp.einsum("stbgio,gh->bsgitho", conv, eye).reshape(nb, chunk * gl * c, chunk * gl * c)

    w_parts = []
    for s in range(chunk):
        wr, wi = _cmul(pw_re[chunk - 1 - s][..., None], pw_im[chunk - 1 - s][..., None], bb_re, bb_im)
        w_parts.append(jnp.stack([wr, wi], 0))
    w = jnp.stack(w_parts, 0).reshape(chunk, 2, nb, gl, n, c)
    w_op = jnp.einsum("srbgni,gh->bsgirhn", w, eye).reshape(nb, chunk * gl * c, 2 * gl * n)

    v_parts = []
    for t in range(chunk):
        dr, di = _cmul(c_re, c_im, pw_re[t + 1][:, None, :], pw_im[t + 1][:, None, :])
        v_parts.append(jnp.stack([dr, -di], 0))
    v = jnp.stack(v_parts, 0).reshape(chunk, 2, nb, gl, c, n)
    v_op = jnp.einsum("trbgon,gh->brgntho", v, eye).reshape(nb, 2 * gl * n, chunk * gl * c)

    def lane(x):
        return x.reshape(nb, gl * n)
    rows = jnp.arange(SUBLANES)[None, :, None]
    tabs = []
    for sh in (1, 2, 4):
        m = (rows >= sh).astype(F32)
        tabs += [lane(pw_re[sh * chunk])[:, None, :] * m, lane(pw_im[sh * chunk])[:, None, :] * m]
    tabs += [jnp.stack([lane(pw_re[(r + 1) * chunk]) for r in range(SUBLANES)], 1),
             jnp.stack([lane(pw_im[(r + 1) * chunk]) for r in range(SUBLANES)], 1)]
    tab = jnp.stack(tabs, 1)

    b_tok = jnp.stack([bb_re, bb_im], 0).reshape(2, nb, gl, n, c)
    b_tok = jnp.einsum("rbgni,gh->bgirhn", b_tok, eye).reshape(nb, gl * c, 2 * gl * n)
    c_tok = jnp.stack([c_re, -c_im], 0).reshape(2, nb, gl, c, n)
    c_tok = jnp.einsum("rbgon,gh->brgnho", c_tok, eye).reshape(nb, 2 * gl * n, gl * c)
    lam_tok = jnp.stack([lane(lb_re), lane(lb_im)], 1)
    return (jnp.concatenate([m_op, w_op], -1).astype(BF16), v_op.astype(BF16), tab,
            b_tok.astype(BF16), c_tok.astype(BF16), lam_tok)


def _gelu_tanh(x):
    return 0.5 * x * (1.0 + jnp.tanh(math.sqrt(2.0 / math.pi) * (x + 0.044715 * (x * x * x))))


def _s5_prompt_body(x_ref, g2_ref, g3_ref, dsk_ref, mw_ref, v_ref, tab_ref, wa_ref, wb_ref,
                    y_ref, sre_ref, sim_ref,
                    uslab, yslab, xg, zscr, escr, state, *, chunk):
    tb, d = x_ref.shape
    nb = d // LANES
    rb = tb // chunk
    cw = chunk * LANES
    sw = state.shape[1] // 2
    tblk = pl.program_id(1)

    @pl.when(tblk == 0)
    def _():
        state[...] = jnp.zeros_like(state)

    u = _rms(x_ref[...], g2_ref[...], EPS)
    for b in range(nb):
        uslab[b] = u[:, b * LANES:(b + 1) * LANES]
    for b in range(nb):
        for t in range(chunk):
            xg[b, :, t * LANES:(t + 1) * LANES] = uslab[b, pl.ds(t, rb, stride=chunk), :].astype(BF16)

    row = lax.broadcasted_iota(jnp.int32, (SUBLANES, sw), 0)

    def block(b, _):
        r = _dot(xg[b], mw_ref[b])
        zscr[...] = r[:, cw:]
        tab = tab_ref[b]
        st = state[pl.ds(b, 1), :]

        def tile(k, carry):
            zp_re, zp_im, ep_re, ep_im = carry
            r0 = pl.multiple_of(k * SUBLANES, SUBLANES)
            z_re = zscr[pl.ds(r0, SUBLANES), :sw]
            z_im = zscr[pl.ds(r0, SUBLANES), sw:]
            s_re = jnp.where(row == 0, zp_re, pltpu.roll(z_re, 1, 0))
            s_im = jnp.where(row == 0, zp_im, pltpu.roll(z_im, 1, 0))
            for lvl, sh in enumerate((1, 2, 4)):
                p_re, p_im = pltpu.roll(s_re, sh, 0), pltpu.roll(s_im, sh, 0)
                t_re, t_im = tab[2 * lvl], tab[2 * lvl + 1]
                s_re, s_im = (s_re + t_re * p_re - t_im * p_im, s_im + t_re * p_im + t_im * p_re)
            c_re, c_im = tab[6], tab[7]
            e_re = s_re + c_re * ep_re - c_im * ep_im
            e_im = s_im + c_re * ep_im + c_im * ep_re
            escr[pl.ds(r0, SUBLANES), :sw] = e_re
            escr[pl.ds(r0, SUBLANES), sw:] = e_im
            last = SUBLANES - 1
            return (z_re[last:, :], z_im[last:, :], e_re[last:, :], e_im[last:, :])

        zero = jnp.zeros((1, sw), F32)
        z_re, z_im, e_re, e_im = lax.fori_loop(0, rb // SUBLANES, tile, (st[:, :sw], st[:, sw:], zero, zero))
        a_re, a_im = tab[6, :1, :], tab[7, :1, :]
        state[pl.ds(b, 1), :sw] = a_re * e_re - a_im * e_im + z_re
        state[pl.ds(b, 1), sw:] = a_re * e_im + a_im * e_re + z_im

        y = r[:, :cw] + _dot(escr[...].astype(BF16), v_ref[b])
        for t in range(chunk):
            yslab[b, pl.ds(t, rb, stride=chunk), :] = y[:, t * LANES:(t + 1) * LANES]
        return 0

    for b in range(nb):
        block(b, 0)

    ys = jnp.concatenate([yslab[b] for b in range(nb)], axis=-1)
    g = _gelu_tanh(ys + dsk_ref[...] * u).astype(BF16)
    o = _dot(g, wa_ref[...]) * jax.nn.sigmoid(_dot(g, wb_ref[...]))
    y_ref[...] = x_ref[...] + _rms(o, g3_ref[...], EPS)

    @pl.when(tblk == pl.num_programs(1) - 1)
    def _():
        sre_ref[...] = state[:, :sw]
        sim_ref[...] = state[:, sw:]


def _s5_prompt_call(x, g2, g3, dsk, mw, v_op, tab, wa, wb, *, tb):
    bsz, seq, d = x.shape
    nb = d // LANES
    chunk = CHUNK
    rb = tb // chunk
    sw = tab.shape[-1]
    body = functools.partial(_s5_prompt_body, chunk=chunk)
    xspec = pl.BlockSpec((None, tb, d), lambda b, t: (b, t, 0))
    sspec = pl.BlockSpec((None, nb, sw), lambda b, t: (b, 0, 0))
    consts = (g2, g3, dsk, mw, v_op, tab, wa, wb)
    return pl.pallas_call(
        body, grid=(bsz, seq // tb),
        in_specs=[xspec] + [_const_spec(a.shape) for a in consts],
        out_specs=[xspec, sspec, sspec],
        out_shape=[jax.ShapeDtypeStruct((bsz, seq, d), F32),
                   jax.ShapeDtypeStruct((bsz, nb, sw), F32), jax.ShapeDtypeStruct((bsz, nb, sw), F32)],
        scratch_shapes=[pltpu.VMEM((nb, tb, LANES), F32), pltpu.VMEM((nb, tb, LANES), F32),
                        pltpu.VMEM((nb, rb, chunk * LANES), BF16),
                        pltpu.VMEM((rb, 2 * sw), F32), pltpu.VMEM((rb, 2 * sw), F32),
                        pltpu.VMEM((nb, 2 * sw), F32)],
        compiler_params=_cparams(("parallel", "arbitrary")), name="s5_prompt")(x, *consts)


def _s5_sample_body(x_ref, hre_ref, him_ref, g2_ref, g3_ref, dsk_ref, bt_ref, ct_ref, lam_ref,
                    wa_ref, wb_ref, y_ref, sre_ref, sim_ref, yscr):
    nt, nseq, d = x_ref.shape
    nb = d // LANES
    sw = lam_ref.shape[-1]
    h_re = [hre_ref[b] for b in range(nb)]
    h_im = [him_ref[b] for b in range(nb)]
    for t in range(nt):
        u = _rms(x_ref[t], g2_ref[...], EPS)
        ub = u.astype(BF16)
        for b in range(nb):
            bu = _dot(ub[:, b * LANES:(b + 1) * LANES], bt_ref[b])
            l_re, l_im = lam_ref[b, 0:1, :], lam_ref[b, 1:2, :]
            n_re = l_re * h_re[b] - l_im * h_im[b] + bu[:, :sw]
            n_im = l_re * h_im[b] + l_im * h_re[b] + bu[:, sw:]
            h_re[b], h_im[b] = n_re, n_im
            hb = jnp.concatenate([n_re, n_im], axis=-1).astype(BF16)
            yscr[t, :, b * LANES:(b + 1) * LANES] = _dot(hb, ct_ref[b])
        yb = yscr[t] + dsk_ref[...] * u
        g = _gelu_tanh(yb).astype(BF16)
        o = _dot(g, wa_ref[...]) * jax.nn.sigmoid(_dot(g, wb_ref[...]))
        y_ref[t] = x_ref[t] + _rms(o, g3_ref[...], EPS)
    for b in range(nb):
        sre_ref[b] = h_re[b]
        sim_ref[b] = h_im[b]


def _s5_sample_call(x, h_re, h_im, g2, g3, dsk, b_tok, c_tok, lam_tok, wa, wb):
    nt, nseq, d = x.shape
    args = (x, h_re, h_im, g2, g3, dsk, b_tok, c_tok, lam_tok, wa, wb)
    return pl.pallas_call(
        _s5_sample_body,
        out_shape=[jax.ShapeDtypeStruct(x.shape, F32), jax.ShapeDtypeStruct(h_re.shape, F32),
                   jax.ShapeDtypeStruct(h_im.shape, F32)],
        scratch_shapes=[pltpu.VMEM((nt, nseq, d), F32)],
        compiler_params=pltpu.CompilerParams(vmem_limit_bytes=VMEM_LIMIT), name="s5_sample")(*args)


def _lambda(lam_ref, lam_init):
    lp = lam_ref[...]
    s1 = jnp.sum(lp[0:1] * lp[1:2], axis=-1, keepdims=True)
    s2 = jnp.sum(lp[2:3] * lp[3:4], axis=-1, keepdims=True)
    return jnp.exp(s1) - jnp.exp(s2) + lam_init


def _attn_prompt_body(q_ref, k_ref, v_ref, lam_ref, sub_ref, o_ref, acc1, acc2, *, tk, scale, lam_init):
    tq, hd2 = q_ref.shape
    hd = hd2 // 2
    i = pl.program_id(2)
    q1, q2 = q_ref[:, :hd], q_ref[:, hd:]
    acc1[...] = jnp.zeros_like(acc1)
    acc2[...] = jnp.zeros_like(acc2)

    def half(qh, kh, vb, m, l, acc, mask):
        s = _dot_nt(qh, kh) * scale
        if mask is not None:
            s = jnp.where(mask, s, NEG)
        m_new = jnp.maximum(m, jnp.max(s, axis=-1, keepdims=True))
        alpha = jnp.exp(m - m_new)
        p = jnp.exp(s - m_new)
        l_new = alpha * l + jnp.sum(p, axis=-1, keepdims=True)
        acc[...] = alpha * acc[...] + _dot(p.astype(BF16), vb)
        return m_new, l_new

    def step(j, carry, mask):
        m1, l1, m2, l2 = carry
        r0 = pl.multiple_of(j * tk, tk)
        kb = k_ref[pl.ds(r0, tk), :]
        vb = v_ref[pl.ds(r0, tk), :]
        m1, l1 = half(q1, kb[:, :hd], vb, m1, l1, acc1, mask)
        m2, l2 = half(q2, kb[:, hd:], vb, m2, l2, acc2, mask)
        return m1, l1, m2, l2

    init = (jnp.full((tq, 1), NEG, F32), jnp.zeros((tq, 1), F32),
            jnp.full((tq, 1), NEG, F32), jnp.zeros((tq, 1), F32))
    n_full = i * (tq // tk)
    carry = lax.fori_loop(0, n_full, lambda j, c: step(j, c, None), init)
    for jj in range(tq // tk):
        rows = lax.broadcasted_iota(jnp.int32, (tq, tk), 0)
        cols = lax.broadcasted_iota(jnp.int32, (tq, tk), 1) + jj * tk
        carry = step(n_full + jj, carry, cols <= rows)
    m1, l1, m2, l2 = carry
    lam = _lambda(lam_ref, lam_init)
    o = acc1[...] / l1 - lam * (acc2[...] / l2)
    o_ref[...] = (_rms(o, sub_ref[...], SUBLN_EPS) * (1.0 - lam_init)).astype(BF16)


def _attn_prompt_call(q, k, v, lam_p, subln, *, bsz, seq, n_heads, tq, tk, lam_init):
    n, d = q.shape
    hw = d // n_heads
    nq = seq // tq
    scale = (hw // 2) ** -0.5
    body = functools.partial(_attn_prompt_body, tk=tk, scale=scale, lam_init=lam_init)
    qspec = pl.BlockSpec((tq, hw), lambda b, h, i: (b * nq + i, h))
    kvspec = pl.BlockSpec((seq, hw), lambda b, h, i: (b, h))
    return pl.pallas_call(
        body, grid=(bsz, n_heads, nq),
        in_specs=[qspec, kvspec, kvspec, _const_spec(lam_p.shape), _const_spec(subln.shape)],
        out_specs=qspec, out_shape=jax.ShapeDtypeStruct((n, d), BF16),
        scratch_shapes=[pltpu.VMEM((tq, hw), F32), pltpu.VMEM((tq, hw), F32)],
        compiler_params=_cparams(("parallel", "parallel", "arbitrary")), name="attn_prompt")(
            q, k, v, lam_p, subln)


def _attn_sample_body(pt_ref, q_ref, bias_ref, knew_ref, vnew_ref, nbias_ref, lam_ref, sub_ref, *rest,
                      pages, scale, lam_init):
    k_refs, v_refs = rest[:pages], rest[pages:2 * pages]
    o_ref, m_scr, l_scr, acc = rest[2 * pages:]
    step = pl.program_id(1)
    nrow = q_ref.shape[0]
    half = nrow // 2

    @pl.when(step == 0)
    def _():
        m_scr[...] = jnp.full_like(m_scr, NEG)
        l_scr[...] = jnp.zeros_like(l_scr)
        acc[...] = jnp.zeros_like(acc)

    q = q_ref[...]

    def scores(kview):
        rows = kview.shape[0] // 2
        parts = []
        for a in range(2):
            ka = kview[pl.ds(a, rows, stride=2), :].astype(BF16)
            parts.append(_dot_nt(q[a * half:(a + 1) * half], ka))
        return jnp.concatenate(parts, axis=0) * scale

    def update(s_list, v_list):
        m = m_scr[...]
        m_new = m
        for s in s_list:
            m_new = jnp.maximum(m_new, jnp.max(s, axis=-1, keepdims=True))
        alpha = jnp.exp(m - m_new)
        l_new = alpha * l_scr[...]
        a_new = alpha * acc[...]
        for s, vview in zip(s_list, v_list):
            p = jnp.exp(s - m_new)
            l_new = l_new + jnp.sum(p, axis=-1, keepdims=True)
            a_new = a_new + _dot(p.astype(BF16), vview.astype(BF16))
        l_scr[...] = l_new
        acc[...] = a_new
        m_scr[...] = m_new

    def vrows(v_ref):
        v = v_ref[...]
        return v.reshape(v.shape[0] * v.shape[1], v.shape[2])

    update([scores(k_refs[pg]) + bias_ref[...] for pg in range(pages)],
           [vrows(v_refs[pg]) for pg in range(pages)])

    @pl.when(step == pl.num_programs(1) - 1)
    def _():
        update([scores(knew_ref) + nbias_ref[...]], [vnew_ref[...]])
        lam = _lambda(lam_ref, lam_init)
        o = acc[...] / l_scr[...]
        o = o[:half] - lam * o[half:]
        o_ref[...] = (_rms(o, sub_ref[...], SUBLN_EPS) * (1.0 - lam_init)).astype(BF16)


def _attn_sample_call(page_table, q_rows, cache_k, cache_v, k_new, v_new, lam_p, subln, *, lam_init):
    nseq, n_pages = page_table.shape
    n_pool, page, nk, hd = cache_k.shape
    _, _, nv, vd = cache_v.shape
    nt = k_new.shape[1]
    cache_k = cache_k.reshape(n_pool, page * nk, hd)
    k_new = k_new.reshape(nseq, nt * nk, hd)
    v_new = v_new.reshape(nseq, nt * nv, vd)
    pages = PAGES_PER_STEP
    nrow = q_rows.shape[1]
    scale = hd ** -0.5
    row_head = (jnp.arange(nrow) // nt) % nv
    row_tok = jnp.arange(nrow) % nt
    lane_head = jnp.arange(page * nv) % nv
    bias = jnp.where(row_head[:, None] == lane_head[None, :], 0.0, NEG).astype(F32)
    nl_head, nl_tok = jnp.arange(nt * nv) % nv, jnp.arange(nt * nv) // nv
    nbias = jnp.where((row_head[:, None] == nl_head[None, :]) & (nl_tok[None, :] <= row_tok[:, None]),
                      0.0, NEG).astype(F32)
    body = functools.partial(_attn_sample_body, pages=pages, scale=scale, lam_init=lam_init)

    def page_spec(shape, pg):
        return pl.BlockSpec((None,) + shape,
                            lambda b, s, pt: (pt[b, s * pages + pg],) + (0,) * len(shape))

    def seq_spec(shape):
        return pl.BlockSpec((None,) + shape, lambda b, s, pt: (b,) + (0,) * len(shape))

    def const(shape):
        return pl.BlockSpec(shape, lambda b, s, pt: (0,) * len(shape))

    in_specs = ([seq_spec(q_rows.shape[1:]), const(bias.shape), seq_spec(k_new.shape[1:]),
                 seq_spec(v_new.shape[1:]), const(nbias.shape), const(lam_p.shape), const(subln.shape)]
                + [page_spec((page * nk, hd), pg) for pg in range(pages)]
                + [page_spec((page, nv, vd), pg) for pg in range(pages)])
    grid_spec = pltpu.PrefetchScalarGridSpec(
        num_scalar_prefetch=1, grid=(nseq, n_pages // pages), in_specs=in_specs,
        out_specs=seq_spec((nrow // 2, vd)),
        scratch_shapes=[pltpu.VMEM((nrow, 1), F32), pltpu.VMEM((nrow, 1), F32), pltpu.VMEM((nrow, vd), F32)])
    return pl.pallas_call(
        body, grid_spec=grid_spec, out_shape=jax.ShapeDtypeStruct((nseq, nrow // 2, vd), BF16),
        compiler_params=_cparams(("parallel", "arbitrary")), name="attn_sample")(
            page_table, q_rows, bias, k_new, v_new, nbias, lam_p, subln,
            *([cache_k] * pages), *([cache_v] * pages))


def _lambda_init(layer):
    return 0.8 - 0.6 * math.exp(-0.3 * layer)


def kernel(x_prompt, x_sample, cache_k, cache_v, state_ssm_re, state_ssm_im, page_table, norm_gains,
           ffn_w_gate, ffn_w_up, ffn_w_down, ssm_a_re, ssm_a_im, ssm_log_dt, ssm_b_re, ssm_b_im,
           ssm_c_re, ssm_c_im, ssm_d, ssm_w_glu_a, ssm_w_glu_b, kv_norm, w_k, w_v, attn_w_q,
           attn_lambda, attn_subln, attn_w_o):
    bsz, seq, d = x_prompt.shape
    nseq, nt, _ = x_sample.shape
    n_kh, hd = cache_k.shape[2:]
    n_vh, vd = cache_v.shape[2:]
    n_groups, n_state = ssm_a_re.shape[1:]
    nb = d // LANES
    sw = GROUPS_PER_BLOCK * n_state

    gains = norm_gains[:, :, None, :]
    wg, wu, wd = ffn_w_gate.astype(BF16), ffn_w_up.astype(BF16), ffn_w_down.astype(BF16)
    wa, wb = ssm_w_glu_a[0].astype(BF16), ssm_w_glu_b[0].astype(BF16)
    wk, wv = w_k.astype(BF16), w_v.astype(BF16)
    wq, wo = attn_w_q[0].astype(BF16), attn_w_o[0].astype(BF16)
    kvn = kv_norm[None, :]
    dsk = ssm_d[0][None, :]
    lam_p, subln = attn_lambda[0], attn_subln[0][None, :]
    lam_init = _lambda_init(1)

    lb_re, lb_im, f_re, f_im = _s5_discretise(ssm_a_re[0], ssm_a_im[0], ssm_log_dt[0])
    mw, v_op, tab, b_tok, c_tok, lam_tok = _s5_operators(
        lb_re, lb_im, f_re, f_im, ssm_b_re[0], ssm_b_im[0], ssm_c_re[0], ssm_c_im[0], CHUNK)

    def ffn(x, layer, idx, tm, **kw):
        return _ffn_call(x, gains[layer, 4 * idx], gains[layer, 4 * idx + 1],
                         wg[layer, idx], wu[layer, idx], wd[layer, idx], tm=tm, **kw)

    def blocks(h):
        return jnp.moveaxis(h.reshape(h.shape[:-2] + (nb, sw)), -2, 0)

    def unblocks(h):
        return jnp.moveaxis(h, 0, -2).reshape(h.shape[1:-1] + (n_groups, n_state))

    def groups(h):
        return h.reshape(h.shape[:-2] + (n_groups, n_state))

    tm_p = 512
    xp = x_prompt.reshape(bsz * seq, d)
    (x1,) = ffn(xp, 0, 0, tm_p)
    x2, sre_p, sim_p = _s5_prompt_call(x1.reshape(bsz, seq, d), gains[0, 2], gains[0, 3], dsk,
                                       mw, v_op, tab, wa, wb, tb=512)
    x3, k32_p, v32_p, k16_p, v16_p = ffn(x2.reshape(bsz * seq, d), 0, 1, tm_p,
                                         kvn=kvn, wk=wk, wv=wv)
    x4, q_p = ffn(x3, 1, 0, tm_p, g_q=gains[1, 2], wq=wq)
    o_p = _attn_prompt_call(q_p, k16_p, v16_p, lam_p, subln, bsz=bsz, seq=seq, n_heads=n_vh,
                            tq=512, tk=512, lam_init=lam_init)
    (y_p,) = ffn(x4, 1, 1, tm_p, o=o_p, wo=wo, g_o=gains[1, 3])

    tm_s = nseq * nt
    xs = x_sample.reshape(nseq * nt, d)
    (s1,) = ffn(xs, 0, 0, tm_s)
    s2, sre_s, sim_s = _s5_sample_call(s1.reshape(nseq, nt, d).transpose(1, 0, 2),
                                       blocks(state_ssm_re[0]), blocks(state_ssm_im[0]),
                                       gains[0, 2], gains[0, 3], dsk, b_tok, c_tok, lam_tok, wa, wb)
    s3, k32_s, v32_s, _, _ = ffn(s2.transpose(1, 0, 2).reshape(nseq * nt, d), 0, 1, tm_s,
                                 kvn=kvn, wk=wk, wv=wv)
    s4, q_s = ffn(s3, 1, 0, tm_s, g_q=gains[1, 2], wq=wq)
    q_rows = q_s.reshape(nseq, nt, n_vh, 2, hd).transpose(0, 3, 2, 1, 4).reshape(nseq, 2 * n_vh * nt, hd)
    k_new = k32_s.reshape(nseq, nt, n_kh, hd)
    v_new = v32_s.reshape(nseq, nt, n_vh, vd)
    o_s = _attn_sample_call(page_table, q_rows, cache_k, cache_v, k_new, v_new, lam_p, subln,
                            lam_init=lam_init)
    o_s = o_s.reshape(nseq, n_vh, nt, vd).transpose(0, 2, 1, 3).reshape(nseq * nt, d)
    (y_s,) = ffn(s4, 1, 1, tm_s, o=o_s, wo=wo, g_o=gains[1, 3])

    return (y_p.reshape(bsz, seq, d), y_s.reshape(nseq, nt, d),
            k32_p.reshape(bsz, seq, n_kh, hd), v32_p.reshape(bsz, seq, n_vh, vd),
            groups(sre_p)[None], groups(sim_p)[None],
            k_new, v_new,
            unblocks(sre_s)[None], unblocks(sim_s)[None])
```

```python
import functools
import math

import jax
import jax.numpy as jnp
from jax import lax
from jax.experimental import pallas as pl
from jax.experimental.pallas import tpu as pltpu

F32 = jnp.float32
BF16 = jnp.bfloat16

EPS = 1e-6
SUBLN_EPS = 1e-5
NEG = -1e30

LANES = 128
SUBLANES = 8
VMEM_LIMIT = 56 * 1024 * 1024

SSM_GROUP = 16
SSM_STATE = 64
GROUPS_PER_BLOCK = LANES // SSM_GROUP
CHUNK = 4
PAGES_PER_STEP = 16
FFN_ROWS = 512
S5_ROWS = 512
ATTN_ROWS = 1024


def _cparams(sem):
    return pltpu.CompilerParams(dimension_semantics=sem, vmem_limit_bytes=VMEM_LIMIT)


def _const_spec(shape):
    nd = len(shape)
    return pl.BlockSpec(shape, lambda *_: (0,) * nd, pipeline_mode=pl.Buffered(1))


def _rms(x, g, eps):
    return x * lax.rsqrt(jnp.mean(x * x, axis=-1, keepdims=True) + eps) * g


def _dot(a, b):
    return jnp.dot(a, b, preferred_element_type=F32)


def _dot_nt(a, b):
    return lax.dot_general(a, b, (((1,), (1,)), ((), ())), preferred_element_type=F32)


def _ffn_body(*refs, has_o, has_kv, has_q, d_ff, f_chunk, v_heads, q_scale):
    it = iter(refs)
    x_ref = next(it)
    if has_o:
        o_ref, wo_ref, g3_ref = next(it), next(it), next(it)
    g0_ref, g1_ref, wg_ref, wu_ref, wd_ref = (next(it) for _ in range(5))
    if has_kv:
        kvn_ref, wk_ref, wv_ref = next(it), next(it), next(it)
    if has_q:
        g2_ref, wq_ref = next(it), next(it)
    y_ref = next(it)
    if has_kv:
        k32_ref, v32_ref, k16_ref, v16_ref = (next(it) for _ in range(4))
    if has_q:
        q_ref = next(it)

    x = x_ref[...]
    if has_o:
        x = x + _rms(_dot(o_ref[...], wo_ref[...]), g3_ref[...], EPS)
    h = _rms(x, g0_ref[...], EPS).astype(BF16)
    acc = None
    for c in range(0, d_ff, f_chunk):
        g = _dot(h, wg_ref[:, c:c + f_chunk])
        u = _dot(h, wu_ref[:, c:c + f_chunk])
        a = (g * jax.nn.sigmoid(g) * u).astype(BF16)
        d = _dot(a, wd_ref[c:c + f_chunk, :])
        acc = d if acc is None else acc + d
    y = x + 0.5 * _rms(acc, g1_ref[...], EPS)
    y_ref[...] = y
    if has_kv:
        kv_in = _rms(y, kvn_ref[...], EPS).astype(BF16)
        k = _dot(kv_in, wk_ref[...])
        v = _dot(kv_in, wv_ref[...])
        tm, d = k.shape
        rows = d // LANES
        for j in range(rows):
            k32_ref[pl.ds(j, tm, stride=rows), :] = k[:, j * LANES:(j + 1) * LANES]
        tiles = rows // v_heads
        for e in range(tiles):
            for hh in range(v_heads):
                c0 = (hh * tiles + e) * LANES
                v32_ref[pl.ds(e * v_heads + hh, tm, stride=rows), :] = v[:, c0:c0 + LANES]
        k16_ref[...] = k.astype(BF16)
        v16_ref[...] = v.astype(BF16)
    if has_q:
        hq = _rms(y, g2_ref[...], EPS).astype(BF16)
        q_ref[...] = (_dot(hq, wq_ref[...]) * q_scale).astype(BF16)


def _ffn_call(x, g_pre, g_post, wg, wu, wd, *, o=None, wo=None, g_o=None,
              kvn=None, wk=None, wv=None, v_heads=None, g_q=None, wq=None, q_scale=None, tm):
    n, d = x.shape
    rows = d // LANES
    d_ff = wg.shape[1]
    has_o, has_kv, has_q = o is not None, wk is not None, wq is not None
    row = pl.BlockSpec((tm, d), lambda i: (i, 0))
    args, specs = [x], [row]
    if has_o:
        args += [o, wo, g_o]
        specs += [row, _const_spec(wo.shape), _const_spec(g_o.shape)]
    args += [g_pre, g_post, wg, wu, wd]
    specs += [_const_spec(a.shape) for a in (g_pre, g_post, wg, wu, wd)]
    out_shape = [jax.ShapeDtypeStruct((n, d), F32)]
    out_specs = [row]
    if has_kv:
        args += [kvn, wk, wv]
        specs += [_const_spec(a.shape) for a in (kvn, wk, wv)]
        page_rows = pl.BlockSpec((tm * rows, LANES), lambda i: (i, 0))
        out_shape += [jax.ShapeDtypeStruct((n * rows, LANES), F32), jax.ShapeDtypeStruct((n * rows, LANES), F32),
                      jax.ShapeDtypeStruct((n, d), BF16), jax.ShapeDtypeStruct((n, d), BF16)]
        out_specs += [page_rows, page_rows, row, row]
    if has_q:
        args += [g_q, wq]
        specs += [_const_spec(g_q.shape), _const_spec(wq.shape)]
        out_shape.append(jax.ShapeDtypeStruct((n, d), BF16))
        out_specs.append(row)
    f_chunk = d_ff // 2 if (d_ff // 2) % LANES == 0 else d_ff
    body = functools.partial(_ffn_body, has_o=has_o, has_kv=has_kv, has_q=has_q,
                             d_ff=d_ff, f_chunk=f_chunk, v_heads=v_heads, q_scale=q_scale)
    return pl.pallas_call(
        body, grid=(n // tm,), in_specs=specs, out_specs=out_specs, out_shape=out_shape,
        compiler_params=_cparams(("parallel",)), name="ffn")(*args)


def _s5_disc_body(are_ref, aim_ref, ldt_ref, lbr_ref, lbi_ref, fr_ref, fi_ref):
    a_re, a_im = are_ref[...], aim_ref[...]
    dt = jnp.exp(ldt_ref[...])
    mag = jnp.exp(a_re * dt)
    lb_re = mag * jnp.cos(a_im * dt)
    lb_im = mag * jnp.sin(a_im * dt)
    num_re = lb_re - 1.0
    den = a_re * a_re + a_im * a_im
    lbr_ref[...] = lb_re
    lbi_ref[...] = lb_im
    fr_ref[...] = (num_re * a_re + lb_im * a_im) / den
    fi_ref[...] = (lb_im * a_re - num_re * a_im) / den


def _s5_discretise(a_re, a_im, log_dt):
    g, n = a_re.shape
    out = jax.ShapeDtypeStruct((g, n), F32)
    return pl.pallas_call(_s5_disc_body, out_shape=[out] * 4, name="s5_disc")(
        a_re, a_im, jnp.broadcast_to(log_dt[:, None], (g, n)))


def _cmul(ar, ai, br, bi):
    return ar * br - ai * bi, ar * bi + ai * br


def _s5_operators(lb_re, lb_im, f_re, f_im, b_re, b_im, c_re, c_im, chunk):
    g, n = lb_re.shape
    nb, gl = g // GROUPS_PER_BLOCK, GROUPS_PER_BLOCK
    c = b_re.shape[-1]
    hi = lax.Precision.HIGHEST
    bb_re = f_re[..., None] * b_re - f_im[..., None] * b_im
    bb_im = f_re[..., None] * b_im + f_im[..., None] * b_re
    pw_re, pw_im = [jnp.ones_like(lb_re)], [jnp.zeros_like(lb_re)]
    for _ in range(8 * chunk):
        r, i = _cmul(pw_re[-1], pw_im[-1], lb_re, lb_im)
        pw_re.append(r)
        pw_im.append(i)
    eye = jnp.eye(gl, dtype=F32)

    taps = []
    for j in range(chunk):
        dr, di = _cmul(c_re, c_im, pw_re[j][:, None, :], pw_im[j][:, None, :])
        taps.append(jnp.einsum("gon,gni->gio", dr, bb_re, precision=hi)
                    - jnp.einsum("gon,gni->gio", di, bb_im, precision=hi))
    zero = jnp.zeros_like(taps[0])
    conv = jnp.stack([jnp.stack([taps[t - s] if t >= s else zero for t in range(chunk)], 0)
                      for s in range(chunk)], 0)
    conv = conv.reshape(chunk, chunk, nb, gl, c, c)
    m_op = jnp.einsum("stbgio,gh->bsgitho", conv, eye).reshape(nb, chunk * gl * c, chunk * gl * c)

    w_parts = []
    for s in range(chunk):
        wr, wi = _cmul(pw_re[chunk - 1 - s][..., None], pw_im[chunk - 1 - s][..., None], bb_re, bb_im)
        w_parts.append(jnp.stack([wr, wi], 0))
    w = jnp.stack(w_parts, 0).reshape(chunk, 2, nb, gl, n, c)
    w_op = jnp.einsum("srbgni,gh->bsgirhn", w, eye).reshape(nb, chunk * gl * c, 2 * gl * n)

    v_parts = []
    for t in range(chunk):
        dr, di = _cmul(c_re, c_im, pw_re[t + 1][:, None, :], pw_im[t + 1][:, None, :])
        v_parts.append(jnp.stack([dr, -di], 0))
    v = jnp.stack(v_parts, 0).reshape(chunk, 2, nb, gl, c, n)
    v_op = jnp.einsum("trbgon,gh->brgntho", v, eye).reshape(nb, 2 * gl * n, chunk * gl * c)

    def lane(x):
        return x.reshape(nb, gl * n)
    rows = jnp.arange(SUBLANES)[None, :, None]
    tabs = []
    for sh in (1, 2, 4):
        m = (rows >= sh).astype(F32)
        tabs += [lane(pw_re[sh * chunk])[:, None, :] * m, lane(pw_im[sh * chunk])[:, None, :] * m]
    tabs += [jnp.stack([lane(pw_re[(r + 1) * chunk]) for r in range(SUBLANES)], 1),
             jnp.stack([lane(pw_im[(r + 1) * chunk]) for r in range(SUBLANES)], 1)]
    tab = jnp.stack(tabs, 1)

    b_tok = jnp.stack([bb_re, bb_im], 0).reshape(2, nb, gl, n, c)
    b_tok = jnp.einsum("rbgni,gh->bgirhn", b_tok, eye).reshape(nb, gl * c, 2 * gl * n)
    c_tok = jnp.stack([c_re, -c_im], 0).reshape(2, nb, gl, c, n)
    c_tok = jnp.einsum("rbgon,gh->brgnho", c_tok, eye).reshape(nb, 2 * gl * n, gl * c)
    lam_tok = jnp.stack([lane(lb_re), lane(lb_im)], 1)
    return (jnp.concatenate([m_op, w_op], -1).astype(BF16), v_op.astype(BF16), tab,
            b_tok.astype(BF16), c_tok.astype(BF16), lam_tok)


def _gelu_tanh(x):
    return 0.5 * x * (1.0 + jnp.tanh(math.sqrt(2.0 / math.pi) * (x + 0.044715 * (x * x * x))))


def _s5_prompt_body(x_ref, g2_ref, g3_ref, dsk_ref, mw_ref, v_ref, tab_ref, wa_ref, wb_ref,
                    y_ref, sre_ref, sim_ref,
                    uslab, yslab, xg, state, *, chunk):
    tb, d = x_ref.shape
    nb = d // LANES
    rb = tb // chunk
    cw = chunk * LANES
    sw = state.shape[1] // 2
    tblk = pl.program_id(1)

    @pl.when(tblk == 0)
    def _():
        state[...] = jnp.zeros_like(state)

    u = _rms(x_ref[...], g2_ref[...], EPS)
    for b in range(nb):
        uslab[b] = u[:, b * LANES:(b + 1) * LANES]
    for b in range(nb):
        for t in range(chunk):
            xg[b, :, t * LANES:(t + 1) * LANES] = uslab[b, pl.ds(t, rb, stride=chunk), :].astype(BF16)

    row = lax.broadcasted_iota(jnp.int32, (SUBLANES, sw), 0)

    last = SUBLANES - 1
    for b in range(nb):
        r = _dot(xg[b], mw_ref[b])
        tab = tab_ref[b]
        zp_re, zp_im = state[b:b + 1, :sw], state[b:b + 1, sw:]
        ep_re = ep_im = jnp.zeros((1, sw), F32)
        e_tiles = []
        for k in range(rb // SUBLANES):
            z_re = r[k * SUBLANES:(k + 1) * SUBLANES, cw:cw + sw]
            z_im = r[k * SUBLANES:(k + 1) * SUBLANES, cw + sw:]
            s_re = jnp.where(row == 0, zp_re, pltpu.roll(z_re, 1, 0))
            s_im = jnp.where(row == 0, zp_im, pltpu.roll(z_im, 1, 0))
            for lvl, sh in enumerate((1, 2, 4)):
                p_re, p_im = pltpu.roll(s_re, sh, 0), pltpu.roll(s_im, sh, 0)
                t_re, t_im = tab[2 * lvl], tab[2 * lvl + 1]
                s_re, s_im = (s_re + t_re * p_re - t_im * p_im, s_im + t_re * p_im + t_im * p_re)
            e_re = s_re + tab[6] * ep_re - tab[7] * ep_im
            e_im = s_im + tab[6] * ep_im + tab[7] * ep_re
            e_tiles.append(jnp.concatenate([e_re, e_im], axis=-1))
            zp_re, zp_im, ep_re, ep_im = z_re[last:, :], z_im[last:, :], e_re[last:, :], e_im[last:, :]
        a_re, a_im = tab[6, :1, :], tab[7, :1, :]
        state[b:b + 1, :sw] = a_re * ep_re - a_im * ep_im + zp_re
        state[b:b + 1, sw:] = a_re * ep_im + a_im * ep_re + zp_im

        y = r[:, :cw] + _dot(jnp.concatenate(e_tiles, axis=0).astype(BF16), v_ref[b])
        for t in range(chunk):
            yslab[b, pl.ds(t, rb, stride=chunk), :] = y[:, t * LANES:(t + 1) * LANES]

    ys = jnp.concatenate([yslab[b] for b in range(nb)], axis=-1)
    g = _gelu_tanh(ys + dsk_ref[...] * u).astype(BF16)
    o = _dot(g, wa_ref[...]) * jax.nn.sigmoid(_dot(g, wb_ref[...]))
    y_ref[...] = x_ref[...] + _rms(o, g3_ref[...], EPS)

    @pl.when(tblk == pl.num_programs(1) - 1)
    def _():
        sre_ref[...] = state[:, :sw]
        sim_ref[...] = state[:, sw:]


def _s5_prompt_call(x, g2, g3, dsk, mw, v_op, tab, wa, wb, *, tb):
    bsz, seq, d = x.shape
    nb = d // LANES
    chunk = CHUNK
    rb = tb // chunk
    sw = tab.shape[-1]
    body = functools.partial(_s5_prompt_body, chunk=chunk)
    xspec = pl.BlockSpec((None, tb, d), lambda b, t: (b, t, 0))
    sspec = pl.BlockSpec((None, nb, sw), lambda b, t: (b, 0, 0))
    consts = (g2, g3, dsk, mw, v_op, tab, wa, wb)
    return pl.pallas_call(
        body, grid=(bsz, seq // tb),
        in_specs=[xspec] + [_const_spec(a.shape) for a in consts],
        out_specs=[xspec, sspec, sspec],
        out_shape=[jax.ShapeDtypeStruct((bsz, seq, d), F32),
                   jax.ShapeDtypeStruct((bsz, nb, sw), F32), jax.ShapeDtypeStruct((bsz, nb, sw), F32)],
        scratch_shapes=[pltpu.VMEM((nb, tb, LANES), F32), pltpu.VMEM((nb, tb, LANES), F32),
                        pltpu.VMEM((nb, rb, chunk * LANES), BF16),
                        pltpu.VMEM((nb, 2 * sw), F32)],
        compiler_params=_cparams(("parallel", "arbitrary")), name="s5_prompt")(x, *consts)


def _s5_sample_body(x_ref, hre_ref, him_ref, g2_ref, g3_ref, dsk_ref, bt_ref, ct_ref, lam_ref,
                    wa_ref, wb_ref, y_ref, sre_ref, sim_ref, yscr):
    nt, nseq, d = x_ref.shape
    nb = d // LANES
    sw = lam_ref.shape[-1]
    h_re = [hre_ref[b] for b in range(nb)]
    h_im = [him_ref[b] for b in range(nb)]
    for t in range(nt):
        u = _rms(x_ref[t], g2_ref[...], EPS)
        ub = u.astype(BF16)
        for b in range(nb):
            bu = _dot(ub[:, b * LANES:(b + 1) * LANES], bt_ref[b])
            l_re, l_im = lam_ref[b, 0:1, :], lam_ref[b, 1:2, :]
            n_re = l_re * h_re[b] - l_im * h_im[b] + bu[:, :sw]
            n_im = l_re * h_im[b] + l_im * h_re[b] + bu[:, sw:]
            h_re[b], h_im[b] = n_re, n_im
            hb = jnp.concatenate([n_re, n_im], axis=-1).astype(BF16)
            yscr[t, :, b * LANES:(b + 1) * LANES] = _dot(hb, ct_ref[b])
        yb = yscr[t] + dsk_ref[...] * u
        g = _gelu_tanh(yb).astype(BF16)
        o = _dot(g, wa_ref[...]) * jax.nn.sigmoid(_dot(g, wb_ref[...]))
        y_ref[t] = x_ref[t] + _rms(o, g3_ref[...], EPS)
    for b in range(nb):
        sre_ref[b] = h_re[b]
        sim_ref[b] = h_im[b]


def _s5_sample_call(x, h_re, h_im, g2, g3, dsk, b_tok, c_tok, lam_tok, wa, wb):
    nt, nseq, d = x.shape
    args = (x, h_re, h_im, g2, g3, dsk, b_tok, c_tok, lam_tok, wa, wb)
    return pl.pallas_call(
        _s5_sample_body,
        out_shape=[jax.ShapeDtypeStruct(x.shape, F32), jax.ShapeDtypeStruct(h_re.shape, F32),
                   jax.ShapeDtypeStruct(h_im.shape, F32)],
        scratch_shapes=[pltpu.VMEM((nt, nseq, d), F32)],
        compiler_params=pltpu.CompilerParams(vmem_limit_bytes=VMEM_LIMIT), name="s5_sample")(*args)


def _lambda(lam_ref, lam_init):
    lp = lam_ref[...]
    s1 = jnp.sum(lp[0:1] * lp[1:2], axis=-1, keepdims=True)
    s2 = jnp.sum(lp[2:3] * lp[3:4], axis=-1, keepdims=True)
    return jnp.exp(s1) - jnp.exp(s2) + lam_init


def _attn_prompt_body(q_ref, k_ref, v_ref, lam_ref, sub_ref, o_ref, acc1, acc2, *, tk, lam_init):
    tq, hd2 = q_ref.shape
    hd = hd2 // 2
    i = pl.program_id(2)
    q1, q2 = q_ref[:, :hd], q_ref[:, hd:]
    acc1[...] = jnp.zeros_like(acc1)
    acc2[...] = jnp.zeros_like(acc2)

    def half(qh, kh, vb, m, l, acc, mask):
        s = _dot_nt(qh, kh)
        if mask is not None:
            s = jnp.where(mask, s, NEG)
        m_new = jnp.maximum(m, jnp.max(s, axis=-1, keepdims=True))
        alpha = jnp.exp2(m - m_new)
        p = jnp.exp2(s - m_new)
        l_new = alpha * l + jnp.sum(p, axis=-1, keepdims=True)
        acc[...] = alpha * acc[...] + _dot(p.astype(BF16), vb)
        return m_new, l_new

    def step(j, carry, masked):
        m1, l1, m2, l2 = carry
        r0 = pl.multiple_of(j * tk, tk)
        kb = k_ref[pl.ds(r0, tk), :]
        vb = v_ref[pl.ds(r0, tk), :]
        mask = None
        if masked:
            rows = lax.broadcasted_iota(jnp.int32, (tq, tk), 0) + i * tq
            cols = lax.broadcasted_iota(jnp.int32, (tq, tk), 1) + j * tk
            mask = cols <= rows
        m1, l1 = half(q1, kb[:, :hd], vb, m1, l1, acc1, mask)
        m2, l2 = half(q2, kb[:, hd:], vb, m2, l2, acc2, mask)
        return m1, l1, m2, l2

    init = (jnp.full((tq, 1), NEG, F32), jnp.zeros((tq, 1), F32),
            jnp.full((tq, 1), NEG, F32), jnp.zeros((tq, 1), F32))
    n_full = (i * tq) // tk
    carry = lax.fori_loop(0, n_full, lambda j, c: step(j, c, False), init)
    for jj in range(max(1, tq // tk)):
        carry = step(n_full + jj, carry, True)
    m1, l1, m2, l2 = carry
    lam = _lambda(lam_ref, lam_init)
    o = acc1[...] / l1 - lam * (acc2[...] / l2)
    o_ref[...] = (_rms(o, sub_ref[...], SUBLN_EPS) * (1.0 - lam_init)).astype(BF16)


def _attn_prompt_call(q, k, v, lam_p, subln, *, bsz, seq, n_heads, tq, tk, lam_init):
    n, d = q.shape
    hw = d // n_heads
    nq = seq // tq
    assert tq % tk == 0 or tk % tq == 0
    body = functools.partial(_attn_prompt_body, tk=tk, lam_init=lam_init)
    qspec = pl.BlockSpec((tq, hw), lambda b, h, i: (b * nq + i, h))
    kvspec = pl.BlockSpec((seq, hw), lambda b, h, i: (b, h))
    return pl.pallas_call(
        body, grid=(bsz, n_heads, nq),
        in_specs=[qspec, kvspec, kvspec, _const_spec(lam_p.shape), _const_spec(subln.shape)],
        out_specs=qspec, out_shape=jax.ShapeDtypeStruct((n, d), BF16),
        scratch_shapes=[pltpu.VMEM((tq, hw), F32), pltpu.VMEM((tq, hw), F32)],
        compiler_params=_cparams(("parallel", "parallel", "arbitrary")), name="attn_prompt")(
            q, k, v, lam_p, subln)


def _attn_sample_body(pt_ref, q_ref, bias_ref, knew_ref, vnew_ref, nbias_ref, lam_ref, sub_ref, *rest,
                      pages, lam_init):
    k_refs, v_refs = rest[:pages], rest[pages:2 * pages]
    o_ref, m_scr, l_scr, acc = rest[2 * pages:]
    step = pl.program_id(1)
    nrow = q_ref.shape[0]
    half = nrow // 2

    @pl.when(step == 0)
    def _():
        m_scr[...] = jnp.full_like(m_scr, NEG)
        l_scr[...] = jnp.zeros_like(l_scr)
        acc[...] = jnp.zeros_like(acc)

    q = q_ref[...]

    def scores(kview):
        rows = kview.shape[0] // 2
        parts = []
        for a in range(2):
            ka = kview[pl.ds(a, rows, stride=2), :].astype(BF16)
            parts.append(_dot_nt(q[a * half:(a + 1) * half], ka))
        return jnp.concatenate(parts, axis=0)

    def update(s_list, v_list):
        m = m_scr[...]
        m_new = m
        for s in s_list:
            m_new = jnp.maximum(m_new, jnp.max(s, axis=-1, keepdims=True))
        alpha = jnp.exp2(m - m_new)
        l_new = alpha * l_scr[...]
        a_new = alpha * acc[...]
        for s, vview in zip(s_list, v_list):
            p = jnp.exp2(s - m_new)
            l_new = l_new + jnp.sum(p, axis=-1, keepdims=True)
            a_new = a_new + _dot(p.astype(BF16), vview.astype(BF16))
        l_scr[...] = l_new
        acc[...] = a_new
        m_scr[...] = m_new

    def vrows(v_ref):
        v = v_ref[...]
        return v.reshape(v.shape[0] * v.shape[1], v.shape[2])

    update([scores(k_refs[pg]) + bias_ref[...] for pg in range(pages)],
           [vrows(v_refs[pg]) for pg in range(pages)])

    @pl.when(step == pl.num_programs(1) - 1)
    def _():
        update([scores(knew_ref) + nbias_ref[...]], [vnew_ref[...]])
        lam = _lambda(lam_ref, lam_init)
        o = acc[...] / l_scr[...]
        o = o[:half] - lam * o[half:]
        o_ref[...] = (_rms(o, sub_ref[...], SUBLN_EPS) * (1.0 - lam_init)).astype(BF16)


def _attn_sample_call(page_table, q_rows, cache_k, cache_v, k_new, v_new, lam_p, subln, *, lam_init):
    nseq, n_pages = page_table.shape
    n_pool, page, nk, hd = cache_k.shape
    _, _, nv, vd = cache_v.shape
    nt = k_new.shape[1]
    cache_k = cache_k.reshape(n_pool, page * nk, hd)
    k_new = k_new.reshape(nseq, nt * nk, hd)
    v_new = v_new.reshape(nseq, nt * nv, vd)
    pages = PAGES_PER_STEP
    nrow = q_rows.shape[1]
    row_head = (jnp.arange(nrow) // nt) % nv
    row_tok = jnp.arange(nrow) % nt
    lane_head = jnp.arange(page * nv) % nv
    bias = jnp.where(row_head[:, None] == lane_head[None, :], 0.0, NEG).astype(F32)
    nl_head, nl_tok = jnp.arange(nt * nv) % nv, jnp.arange(nt * nv) // nv
    nbias = jnp.where((row_head[:, None] == nl_head[None, :]) & (nl_tok[None, :] <= row_tok[:, None]),
                      0.0, NEG).astype(F32)
    body = functools.partial(_attn_sample_body, pages=pages, lam_init=lam_init)

    def page_spec(shape, pg):
        return pl.BlockSpec((None,) + shape,
                            lambda b, s, pt: (pt[b, s * pages + pg],) + (0,) * len(shape))

    def seq_spec(shape):
        return pl.BlockSpec((None,) + shape, lambda b, s, pt: (b,) + (0,) * len(shape))

    def const(shape):
        return pl.BlockSpec(shape, lambda b, s, pt: (0,) * len(shape))

    in_specs = ([seq_spec(q_rows.shape[1:]), const(bias.shape), seq_spec(k_new.shape[1:]),
                 seq_spec(v_new.shape[1:]), const(nbias.shape), const(lam_p.shape), const(subln.shape)]
                + [page_spec((page * nk, hd), pg) for pg in range(pages)]
                + [page_spec((page, nv, vd), pg) for pg in range(pages)])
    grid_spec = pltpu.PrefetchScalarGridSpec(
        num_scalar_prefetch=1, grid=(nseq, n_pages // pages), in_specs=in_specs,
        out_specs=seq_spec((nrow // 2, vd)),
        scratch_shapes=[pltpu.VMEM((nrow, 1), F32), pltpu.VMEM((nrow, 1), F32), pltpu.VMEM((nrow, vd), F32)])
    return pl.pallas_call(
        body, grid_spec=grid_spec, out_shape=jax.ShapeDtypeStruct((nseq, nrow // 2, vd), BF16),
        compiler_params=_cparams(("parallel", "arbitrary")), name="attn_sample")(
            page_table, q_rows, bias, k_new, v_new, nbias, lam_p, subln,
            *([cache_k] * pages), *([cache_v] * pages))


def _lambda_init(layer):
    return 0.8 - 0.6 * math.exp(-0.3 * layer)


def kernel(x_prompt, x_sample, cache_k, cache_v, state_ssm_re, state_ssm_im, page_table, norm_gains,
           ffn_w_gate, ffn_w_up, ffn_w_down, ssm_a_re, ssm_a_im, ssm_log_dt, ssm_b_re, ssm_b_im,
           ssm_c_re, ssm_c_im, ssm_d, ssm_w_glu_a, ssm_w_glu_b, kv_norm, w_k, w_v, attn_w_q,
           attn_lambda, attn_subln, attn_w_o):
    bsz, seq, d = x_prompt.shape
    nseq, nt, _ = x_sample.shape
    n_kh, hd = cache_k.shape[2:]
    n_vh, vd = cache_v.shape[2:]
    n_groups, n_state = ssm_a_re.shape[1:]
    nb = d // LANES
    sw = GROUPS_PER_BLOCK * n_state

    gains = norm_gains[:, :, None, :]
    wg, wu, wd = ffn_w_gate.astype(BF16), ffn_w_up.astype(BF16), ffn_w_down.astype(BF16)
    wa, wb = ssm_w_glu_a[0].astype(BF16), ssm_w_glu_b[0].astype(BF16)
    wk, wv = w_k.astype(BF16), w_v.astype(BF16)
    wq, wo = attn_w_q[0].astype(BF16), attn_w_o[0].astype(BF16)
    kvn = kv_norm[None, :]
    dsk = ssm_d[0][None, :]
    lam_p, subln = attn_lambda[0], attn_subln[0][None, :]
    lam_init = _lambda_init(1)

    lb_re, lb_im, f_re, f_im = _s5_discretise(ssm_a_re[0], ssm_a_im[0], ssm_log_dt[0])
    mw, v_op, tab, b_tok, c_tok, lam_tok = _s5_operators(
        lb_re, lb_im, f_re, f_im, ssm_b_re[0], ssm_b_im[0], ssm_c_re[0], ssm_c_im[0], CHUNK)

    def ffn(x, layer, idx, tm, **kw):
        return _ffn_call(x, gains[layer, 4 * idx], gains[layer, 4 * idx + 1],
                         wg[layer, idx], wu[layer, idx], wd[layer, idx], tm=tm, **kw)

    def blocks(h):
        return jnp.moveaxis(h.reshape(h.shape[:-2] + (nb, sw)), -2, 0)

    def unblocks(h):
        return jnp.moveaxis(h, 0, -2).reshape(h.shape[1:-1] + (n_groups, n_state))

    def groups(h):
        return h.reshape(h.shape[:-2] + (n_groups, n_state))

    q_scale = hd ** -0.5 * math.log2(math.e)
    kv_args = dict(kvn=kvn, wk=wk, wv=wv, v_heads=n_vh)
    q_args = dict(g_q=gains[1, 2], wq=wq, q_scale=q_scale)

    def k_rows(r, lead):
        return r.reshape(lead + (n_kh, hd))

    def v_rows(r, lead):
        tiles = vd // LANES
        return r.reshape(lead + (tiles, n_vh, LANES)).swapaxes(-3, -2).reshape(lead + (n_vh, vd))

    tm_p = min(FFN_ROWS, bsz * seq)
    xp = x_prompt.reshape(bsz * seq, d)
    (x1,) = ffn(xp, 0, 0, tm_p)
    x2, sre_p, sim_p = _s5_prompt_call(x1.reshape(bsz, seq, d), gains[0, 2], gains[0, 3], dsk,
                                       mw, v_op, tab, wa, wb, tb=min(S5_ROWS, seq))
    x3, k32_p, v32_p, k16_p, v16_p = ffn(x2.reshape(bsz * seq, d), 0, 1, tm_p, **kv_args)
    x4, q_p = ffn(x3, 1, 0, tm_p, **q_args)
    t_attn = min(ATTN_ROWS, seq)
    o_p = _attn_prompt_call(q_p, k16_p, v16_p, lam_p, subln, bsz=bsz, seq=seq, n_heads=n_vh,
                            tq=t_attn, tk=t_attn, lam_init=lam_init)
    (y_p,) = ffn(x4, 1, 1, tm_p, o=o_p, wo=wo, g_o=gains[1, 3])

    tm_s = nseq * nt
    xs = x_sample.reshape(nseq * nt, d)
    (s1,) = ffn(xs, 0, 0, tm_s)
    s2, sre_s, sim_s = _s5_sample_call(s1.reshape(nseq, nt, d).transpose(1, 0, 2),
                                       blocks(state_ssm_re[0]), blocks(state_ssm_im[0]),
                                       gains[0, 2], gains[0, 3], dsk, b_tok, c_tok, lam_tok, wa, wb)
    s3, k32_s, v32_s, _, _ = ffn(s2.transpose(1, 0, 2).reshape(nseq * nt, d), 0, 1, tm_s, **kv_args)
    s4, q_s = ffn(s3, 1, 0, tm_s, **q_args)
    q_rows = q_s.reshape(nseq, nt, n_vh, 2, hd).transpose(0, 3, 2, 1, 4).reshape(nseq, 2 * n_vh * nt, hd)
    k_new = k_rows(k32_s, (nseq, nt))
    v_new = v_rows(v32_s, (nseq, nt))
    o_s = _attn_sample_call(page_table, q_rows, cache_k, cache_v, k_new, v_new, lam_p, subln,
                            lam_init=lam_init)
    o_s = o_s.reshape(nseq, n_vh, nt, vd).transpose(0, 2, 1, 3).reshape(nseq * nt, d)
    (y_s,) = ffn(s4, 1, 1, tm_s, o=o_s, wo=wo, g_o=gains[1, 3])

    return (y_p.reshape(bsz, seq, d), y_s.reshape(nseq, nt, d),
            k_rows(k32_p, (bsz, seq)), v_rows(v32_p, (bsz, seq)),
            groups(sre_p)[None], groups(sim_p)[None],
            k_new, v_new,
            unblocks(sre_s)[None], unblocks(sim_s)[None])
```

```python
import functools
import math

import jax
import jax.numpy as jnp
from jax import lax
from jax.experimental import pallas as pl
from jax.experimental.pallas import tpu as pltpu

F32 = jnp.float32
BF16 = jnp.bfloat16

EPS = 1e-6
SUBLN_EPS = 1e-5
NEG = -1e30

LANES = 128
SUBLANES = 8
VMEM_LIMIT = 56 * 1024 * 1024

SSM_GROUP = 16
SSM_STATE = 64
GROUPS_PER_BLOCK = LANES // SSM_GROUP
CHUNK = 4
PAGES_PER_STEP = 16
FFN_ROWS = 512
S5_ROWS = 512
ATTN_ROWS = 1024


def _cparams(sem):
    return pltpu.CompilerParams(dimension_semantics=sem, vmem_limit_bytes=VMEM_LIMIT)


def _const_spec(shape):
    nd = len(shape)
    return pl.BlockSpec(shape, lambda *_: (0,) * nd, pipeline_mode=pl.Buffered(1))


def _rms(x, g, eps):
    return x * lax.rsqrt(jnp.mean(x * x, axis=-1, keepdims=True) + eps) * g


def _dot(a, b):
    return jnp.dot(a, b, preferred_element_type=F32)


def _dot_nt(a, b):
    return lax.dot_general(a, b, (((1,), (1,)), ((), ())), preferred_element_type=F32)


def _ffn_body(*refs, has_o, has_kv, has_q, d_ff, f_chunk, v_heads, q_scale):
    it = iter(refs)
    x_ref = next(it)
    if has_o:
        o_ref, wo_ref, g3_ref = next(it), next(it), next(it)
    g0_ref, g1_ref, wg_ref, wu_ref, wd_ref = (next(it) for _ in range(5))
    if has_kv:
        kvn_ref, wk_ref, wv_ref = next(it), next(it), next(it)
    if has_q:
        g2_ref, wq_ref = next(it), next(it)
    y_ref = next(it)
    if has_kv:
        k32_ref, v32_ref, k16_ref, v16_ref = (next(it) for _ in range(4))
    if has_q:
        q_ref = next(it)

    x = x_ref[...]
    if has_o:
        x = x + _rms(_dot(o_ref[...], wo_ref[...]), g3_ref[...], EPS)
    h = _rms(x, g0_ref[...], EPS).astype(BF16)
    acc = None
    for c in range(0, d_ff, f_chunk):
        g = _dot(h, wg_ref[:, c:c + f_chunk])
        u = _dot(h, wu_ref[:, c:c + f_chunk])
        a = (g * jax.nn.sigmoid(g) * u).astype(BF16)
        d = _dot(a, wd_ref[c:c + f_chunk, :])
        acc = d if acc is None else acc + d
    y = x + 0.5 * _rms(acc, g1_ref[...], EPS)
    y_ref[...] = y
    if has_kv:
        kv_in = _rms(y, kvn_ref[...], EPS).astype(BF16)
        k = _dot(kv_in, wk_ref[...])
        v = _dot(kv_in, wv_ref[...])
        tm, d = k.shape
        rows = d // LANES
        for j in range(rows):
            k32_ref[pl.ds(j, tm, stride=rows), :] = k[:, j * LANES:(j + 1) * LANES]
        tiles = rows // v_heads
        for e in range(tiles):
            for hh in range(v_heads):
                c0 = (hh * tiles + e) * LANES
                v32_ref[pl.ds(e * v_heads + hh, tm, stride=rows), :] = v[:, c0:c0 + LANES]
        k16_ref[...] = k.astype(BF16)
        v16_ref[...] = v.astype(BF16)
    if has_q:
        hq = _rms(y, g2_ref[...], EPS).astype(BF16)
        q_ref[...] = (_dot(hq, wq_ref[...]) * q_scale).astype(BF16)


def _ffn_call(x, g_pre, g_post, wg, wu, wd, *, o=None, wo=None, g_o=None,
              kvn=None, wk=None, wv=None, v_heads=None, g_q=None, wq=None, q_scale=None, tm):
    n, d = x.shape
    rows = d // LANES
    d_ff = wg.shape[1]
    has_o, has_kv, has_q = o is not None, wk is not None, wq is not None
    row = pl.BlockSpec((tm, d), lambda i: (i, 0))
    args, specs = [x], [row]
    if has_o:
        args += [o, wo, g_o]
        specs += [row, _const_spec(wo.shape), _const_spec(g_o.shape)]
    args += [g_pre, g_post, wg, wu, wd]
    specs += [_const_spec(a.shape) for a in (g_pre, g_post, wg, wu, wd)]
    out_shape = [jax.ShapeDtypeStruct((n, d), F32)]
    out_specs = [row]
    if has_kv:
        args += [kvn, wk, wv]
        specs += [_const_spec(a.shape) for a in (kvn, wk, wv)]
        page_rows = pl.BlockSpec((tm * rows, LANES), lambda i: (i, 0))
        out_shape += [jax.ShapeDtypeStruct((n * rows, LANES), F32), jax.ShapeDtypeStruct((n * rows, LANES), F32),
                      jax.ShapeDtypeStruct((n, d), BF16), jax.ShapeDtypeStruct((n, d), BF16)]
        out_specs += [page_rows, page_rows, row, row]
    if has_q:
        args += [g_q, wq]
        specs += [_const_spec(g_q.shape), _const_spec(wq.shape)]
        out_shape.append(jax.ShapeDtypeStruct((n, d), BF16))
        out_specs.append(row)
    f_chunk = d_ff // 2 if (d_ff // 2) % LANES == 0 else d_ff
    body = functools.partial(_ffn_body, has_o=has_o, has_kv=has_kv, has_q=has_q,
                             d_ff=d_ff, f_chunk=f_chunk, v_heads=v_heads, q_scale=q_scale)
    return pl.pallas_call(
        body, grid=(n // tm,), in_specs=specs, out_specs=out_specs, out_shape=out_shape,
        compiler_params=_cparams(("parallel",)), name="ffn")(*args)


def _s5_disc_body(are_ref, aim_ref, ldt_ref, lbr_ref, lbi_ref, fr_ref, fi_ref):
    a_re, a_im = are_ref[...], aim_ref[...]
    dt = jnp.exp(ldt_ref[...])
    mag = jnp.exp(a_re * dt)
    lb_re = mag * jnp.cos(a_im * dt)
    lb_im = mag * jnp.sin(a_im * dt)
    num_re = lb_re - 1.0
    den = a_re * a_re + a_im * a_im
    lbr_ref[...] = lb_re
    lbi_ref[...] = lb_im
    fr_ref[...] = (num_re * a_re + lb_im * a_im) / den
    fi_ref[...] = (lb_im * a_re - num_re * a_im) / den


def _s5_discretise(a_re, a_im, log_dt):
    g, n = a_re.shape
    out = jax.ShapeDtypeStruct((g, n), F32)
    return pl.pallas_call(_s5_disc_body, out_shape=[out] * 4, name="s5_disc")(
        a_re, a_im, jnp.broadcast_to(log_dt[:, None], (g, n)))


def _cmul(ar, ai, br, bi):
    return ar * br - ai * bi, ar * bi + ai * br


def _s5_operators(lb_re, lb_im, f_re, f_im, b_re, b_im, c_re, c_im, chunk):
    g, n = lb_re.shape
    nb, gl = g // GROUPS_PER_BLOCK, GROUPS_PER_BLOCK
    c = b_re.shape[-1]
    bb_re = f_re[..., None] * b_re - f_im[..., None] * b_im
    bb_im = f_re[..., None] * b_im + f_im[..., None] * b_re
    pw_re, pw_im = [jnp.ones_like(lb_re)], [jnp.zeros_like(lb_re)]
    for _ in range(8 * chunk):
        r, i = _cmul(pw_re[-1], pw_im[-1], lb_re, lb_im)
        pw_re.append(r)
        pw_im.append(i)
    eye = jnp.eye(gl, dtype=F32)

    def block_diag(x, perm, group_axis, n_col_axes):
        y = jnp.expand_dims(x.transpose(perm), -2)
        shape = [1] * y.ndim
        shape[group_axis], shape[-2] = gl, gl
        y = y * eye.reshape(shape)
        n_rows = math.prod(y.shape[1:y.ndim - n_col_axes - 1])
        return y.reshape(nb, n_rows, -1)

    taps = []
    for j in range(chunk):
        dr, di = _cmul(c_re, c_im, pw_re[j][:, None, :], pw_im[j][:, None, :])
        tap = jnp.sum(dr[:, :, :, None] * bb_re[:, None, :, :] - di[:, :, :, None] * bb_im[:, None, :, :],
                      axis=2)
        taps.append(tap.swapaxes(1, 2))
    zero = jnp.zeros_like(taps[0])
    conv = jnp.stack([jnp.stack([taps[t - s] if t >= s else zero for t in range(chunk)], 0)
                      for s in range(chunk)], 0)
    conv = conv.reshape(chunk, chunk, nb, gl, c, c)
    m_op = block_diag(conv, (2, 0, 3, 4, 1, 5), 2, 2)

    w_parts = []
    for s in range(chunk):
        wr, wi = _cmul(pw_re[chunk - 1 - s][..., None], pw_im[chunk - 1 - s][..., None], bb_re, bb_im)
        w_parts.append(jnp.stack([wr, wi], 0))
    w = jnp.stack(w_parts, 0).reshape(chunk, 2, nb, gl, n, c)
    w_op = block_diag(w, (2, 0, 3, 5, 1, 4), 2, 2)

    v_parts = []
    for t in range(chunk):
        dr, di = _cmul(c_re, c_im, pw_re[t + 1][:, None, :], pw_im[t + 1][:, None, :])
        v_parts.append(jnp.stack([dr, -di], 0))
    v = jnp.stack(v_parts, 0).reshape(chunk, 2, nb, gl, c, n)
    v_op = block_diag(v, (2, 1, 3, 5, 0, 4), 2, 2)

    def lane(x):
        return x.reshape(nb, gl * n)
    rows = jnp.arange(SUBLANES)[None, :, None]
    tabs = []
    for sh in (1, 2, 4):
        m = (rows >= sh).astype(F32)
        tabs += [lane(pw_re[sh * chunk])[:, None, :] * m, lane(pw_im[sh * chunk])[:, None, :] * m]
    tabs += [jnp.stack([lane(pw_re[(r + 1) * chunk]) for r in range(SUBLANES)], 1),
             jnp.stack([lane(pw_im[(r + 1) * chunk]) for r in range(SUBLANES)], 1)]
    tab = jnp.stack(tabs, 1)

    b_tok = jnp.stack([bb_re, bb_im], 0).reshape(2, nb, gl, n, c)
    b_tok = block_diag(b_tok, (1, 2, 4, 0, 3), 1, 2)
    c_tok = jnp.stack([c_re, -c_im], 0).reshape(2, nb, gl, c, n)
    c_tok = block_diag(c_tok, (1, 0, 2, 4, 3), 2, 1)
    lam_tok = jnp.stack([lane(lb_re), lane(lb_im)], 1)
    return (jnp.concatenate([m_op, w_op], -1).astype(BF16), v_op.astype(BF16), tab,
            b_tok.astype(BF16), c_tok.astype(BF16), lam_tok)


def _gelu_tanh(x):
    return 0.5 * x * (1.0 + jnp.tanh(math.sqrt(2.0 / math.pi) * (x + 0.044715 * (x * x * x))))


def _s5_prompt_body(x_ref, g2_ref, g3_ref, dsk_ref, mw_ref, v_ref, tab_ref, wa_ref, wb_ref,
                    y_ref, sre_ref, sim_ref,
                    uslab, yslab, xg, state, *, chunk):
    tb, d = x_ref.shape
    nb = d // LANES
    rb = tb // chunk
    cw = chunk * LANES
    sw = state.shape[1] // 2
    tblk = pl.program_id(1)

    @pl.when(tblk == 0)
    def _():
        state[...] = jnp.zeros_like(state)

    u = _rms(x_ref[...], g2_ref[...], EPS)
    for b in range(nb):
        uslab[b] = u[:, b * LANES:(b + 1) * LANES]
    for b in range(nb):
        for t in range(chunk):
            xg[b, :, t * LANES:(t + 1) * LANES] = uslab[b, pl.ds(t, rb, stride=chunk), :].astype(BF16)

    row = lax.broadcasted_iota(jnp.int32, (SUBLANES, sw), 0)

    last = SUBLANES - 1
    for b in range(nb):
        r = _dot(xg[b], mw_ref[b])
        tab = tab_ref[b]
        zp_re, zp_im = state[b:b + 1, :sw], state[b:b + 1, sw:]
        ep_re = ep_im = jnp.zeros((1, sw), F32)
        e_tiles = []
        for k in range(rb // SUBLANES):
            z_re = r[k * SUBLANES:(k + 1) * SUBLANES, cw:cw + sw]
            z_im = r[k * SUBLANES:(k + 1) * SUBLANES, cw + sw:]
            s_re = jnp.where(row == 0, zp_re, pltpu.roll(z_re, 1, 0))
            s_im = jnp.where(row == 0, zp_im, pltpu.roll(z_im, 1, 0))
            for lvl, sh in enumerate((1, 2, 4)):
                p_re, p_im = pltpu.roll(s_re, sh, 0), pltpu.roll(s_im, sh, 0)
                t_re, t_im = tab[2 * lvl], tab[2 * lvl + 1]
                s_re, s_im = (s_re + t_re * p_re - t_im * p_im, s_im + t_re * p_im + t_im * p_re)
            e_re = s_re + tab[6] * ep_re - tab[7] * ep_im
            e_im = s_im + tab[6] * ep_im + tab[7] * ep_re
            e_tiles.append(jnp.concatenate([e_re, e_im], axis=-1))
            zp_re, zp_im, ep_re, ep_im = z_re[last:, :], z_im[last:, :], e_re[last:, :], e_im[last:, :]
        a_re, a_im = tab[6, :1, :], tab[7, :1, :]
        state[b:b + 1, :sw] = a_re * ep_re - a_im * ep_im + zp_re
        state[b:b + 1, sw:] = a_re * ep_im + a_im * ep_re + zp_im

        y = r[:, :cw] + _dot(jnp.concatenate(e_tiles, axis=0).astype(BF16), v_ref[b])
        for t in range(chunk):
            yslab[b, pl.ds(t, rb, stride=chunk), :] = y[:, t * LANES:(t + 1) * LANES]

    ys = jnp.concatenate([yslab[b] for b in range(nb)], axis=-1)
    g = _gelu_tanh(ys + dsk_ref[...] * u).astype(BF16)
    o = _dot(g, wa_ref[...]) * jax.nn.sigmoid(_dot(g, wb_ref[...]))
    y_ref[...] = x_ref[...] + _rms(o, g3_ref[...], EPS)

    @pl.when(tblk == pl.num_programs(1) - 1)
    def _():
        sre_ref[...] = state[:, :sw]
        sim_ref[...] = state[:, sw:]


def _s5_prompt_call(x, g2, g3, dsk, mw, v_op, tab, wa, wb, *, tb):
    bsz, seq, d = x.shape
    nb = d // LANES
    chunk = CHUNK
    rb = tb // chunk
    sw = tab.shape[-1]
    body = functools.partial(_s5_prompt_body, chunk=chunk)
    xspec = pl.BlockSpec((None, tb, d), lambda b, t: (b, t, 0))
    sspec = pl.BlockSpec((None, nb, sw), lambda b, t: (b, 0, 0))
    consts = (g2, g3, dsk, mw, v_op, tab, wa, wb)
    return pl.pallas_call(
        body, grid=(bsz, seq // tb),
        in_specs=[xspec] + [_const_spec(a.shape) for a in consts],
        out_specs=[xspec, sspec, sspec],
        out_shape=[jax.ShapeDtypeStruct((bsz, seq, d), F32),
                   jax.ShapeDtypeStruct((bsz, nb, sw), F32), jax.ShapeDtypeStruct((bsz, nb, sw), F32)],
        scratch_shapes=[pltpu.VMEM((nb, tb, LANES), F32), pltpu.VMEM((nb, tb, LANES), F32),
                        pltpu.VMEM((nb, rb, chunk * LANES), BF16),
                        pltpu.VMEM((nb, 2 * sw), F32)],
        compiler_params=_cparams(("parallel", "arbitrary")), name="s5_prompt")(x, *consts)


def _s5_sample_body(x_ref, hre_ref, him_ref, g2_ref, g3_ref, dsk_ref, bt_ref, ct_ref, lam_ref,
                    wa_ref, wb_ref, y_ref, sre_ref, sim_ref, yscr):
    nt, nseq, d = x_ref.shape
    nb = d // LANES
    sw = lam_ref.shape[-1]
    h_re = [hre_ref[b] for b in range(nb)]
    h_im = [him_ref[b] for b in range(nb)]
    for t in range(nt):
        u = _rms(x_ref[t], g2_ref[...], EPS)
        ub = u.astype(BF16)
        for b in range(nb):
            bu = _dot(ub[:, b * LANES:(b + 1) * LANES], bt_ref[b])
            l_re, l_im = lam_ref[b, 0:1, :], lam_ref[b, 1:2, :]
            n_re = l_re * h_re[b] - l_im * h_im[b] + bu[:, :sw]
            n_im = l_re * h_im[b] + l_im * h_re[b] + bu[:, sw:]
            h_re[b], h_im[b] = n_re, n_im
            hb = jnp.concatenate([n_re, n_im], axis=-1).astype(BF16)
            yscr[t, :, b * LANES:(b + 1) * LANES] = _dot(hb, ct_ref[b])
        yb = yscr[t] + dsk_ref[...] * u
        g = _gelu_tanh(yb).astype(BF16)
        o = _dot(g, wa_ref[...]) * jax.nn.sigmoid(_dot(g, wb_ref[...]))
        y_ref[t] = x_ref[t] + _rms(o, g3_ref[...], EPS)
    for b in range(nb):
        sre_ref[b] = h_re[b]
        sim_ref[b] = h_im[b]


def _s5_sample_call(x, h_re, h_im, g2, g3, dsk, b_tok, c_tok, lam_tok, wa, wb):
    nt, nseq, d = x.shape
    args = (x, h_re, h_im, g2, g3, dsk, b_tok, c_tok, lam_tok, wa, wb)
    return pl.pallas_call(
        _s5_sample_body,
        out_shape=[jax.ShapeDtypeStruct(x.shape, F32), jax.ShapeDtypeStruct(h_re.shape, F32),
                   jax.ShapeDtypeStruct(h_im.shape, F32)],
        scratch_shapes=[pltpu.VMEM((nt, nseq, d), F32)],
        compiler_params=pltpu.CompilerParams(vmem_limit_bytes=VMEM_LIMIT), name="s5_sample")(*args)


def _lambda(lam_ref, lam_init):
    lp = lam_ref[...]
    s1 = jnp.sum(lp[0:1] * lp[1:2], axis=-1, keepdims=True)
    s2 = jnp.sum(lp[2:3] * lp[3:4], axis=-1, keepdims=True)
    return jnp.exp(s1) - jnp.exp(s2) + lam_init


def _attn_prompt_body(q_ref, k_ref, v_ref, lam_ref, sub_ref, o_ref, s_scr, acc, *, lam_init):
    tq, hd2 = q_ref.shape
    hd = hd2 // 2
    hq = tq // 2
    i = pl.program_id(2)
    acc[...] = jnp.zeros_like(acc)

    def update(h, s, vb, m, l, rows):
        m_new = jnp.maximum(m, jnp.max(s, axis=-1, keepdims=True))
        alpha = jnp.exp2(m - m_new)
        p = jnp.exp2(s - m_new)
        l_new = alpha * l + jnp.sum(p, axis=-1, keepdims=True)
        acc[h, rows, :] = alpha * acc[h, rows, :] + _dot(p.astype(BF16), vb)
        return m_new, l_new

    def full_block(j, carry):
        r0 = pl.multiple_of(j * tq, tq)
        kb = k_ref[pl.ds(r0, tq), :]
        vb = v_ref[pl.ds(r0, tq), :]
        for h in range(2):
            s_scr[h] = _dot_nt(q_ref[:, h * hd:(h + 1) * hd], kb[:, h * hd:(h + 1) * hd])
        out = []
        for h in range(2):
            out += update(h, s_scr[h], vb, carry[2 * h], carry[2 * h + 1], slice(None))
        return tuple(out)

    init = (jnp.full((tq, 1), NEG, F32), jnp.zeros((tq, 1), F32)) * 2
    carry = lax.fori_loop(0, i, full_block, init)

    d0 = pl.multiple_of(i * tq, tq)
    out = []
    for h in range(2):
        m, l = carry[2 * h], carry[2 * h + 1]
        stats = []
        for r0, nk in ((0, hq), (hq, tq)):
            kb = k_ref[pl.ds(d0, nk), h * hd:(h + 1) * hd]
            vb = v_ref[pl.ds(d0, nk), :]
            s = _dot_nt(q_ref[r0:r0 + hq, h * hd:(h + 1) * hd], kb)
            row = lax.broadcasted_iota(jnp.int32, (hq, nk), 0) + r0
            col = lax.broadcasted_iota(jnp.int32, (hq, nk), 1)
            s = jnp.where(col <= row, s, NEG)
            stats.append(update(h, s, vb, m[r0:r0 + hq], l[r0:r0 + hq], slice(r0, r0 + hq)))
        out.append(jnp.concatenate([stats[0][1], stats[1][1]], axis=0))
    l1, l2 = out
    lam = _lambda(lam_ref, lam_init)
    o = acc[0] / l1 - lam * (acc[1] / l2)
    o_ref[...] = (_rms(o, sub_ref[...], SUBLN_EPS) * (1.0 - lam_init)).astype(BF16)


def _attn_prompt_call(q, k, v, lam_p, subln, *, bsz, seq, n_heads, tq, lam_init):
    n, d = q.shape
    hw = d // n_heads
    nq = seq // tq
    body = functools.partial(_attn_prompt_body, lam_init=lam_init)
    qspec = pl.BlockSpec((tq, hw), lambda b, h, i: (b * nq + i, h))
    kvspec = pl.BlockSpec((seq, hw), lambda b, h, i: (b, h))
    return pl.pallas_call(
        body, grid=(bsz, n_heads, nq),
        in_specs=[qspec, kvspec, kvspec, _const_spec(lam_p.shape), _const_spec(subln.shape)],
        out_specs=qspec, out_shape=jax.ShapeDtypeStruct((n, d), BF16),
        scratch_shapes=[pltpu.VMEM((2, tq, tq), F32), pltpu.VMEM((2, tq, hw), F32)],
        compiler_params=_cparams(("parallel", "parallel", "arbitrary")), name="attn_prompt")(
            q, k, v, lam_p, subln)


def _attn_sample_body(pt_ref, q_ref, bias_ref, knew_ref, vnew_ref, nbias_ref, lam_ref, sub_ref, *rest,
                      pages, lam_init):
    k_refs, v_refs = rest[:pages], rest[pages:2 * pages]
    o_ref, m_scr, l_scr, acc = rest[2 * pages:]
    step = pl.program_id(1)
    nrow = q_ref.shape[0]
    half = nrow // 2

    @pl.when(step == 0)
    def _():
        m_scr[...] = jnp.full_like(m_scr, NEG)
        l_scr[...] = jnp.zeros_like(l_scr)
        acc[...] = jnp.zeros_like(acc)

    q = q_ref[...]

    def scores(kview):
        rows = kview.shape[0] // 2
        parts = []
        for a in range(2):
            ka = kview[pl.ds(a, rows, stride=2), :].astype(BF16)
            parts.append(_dot_nt(q[a * half:(a + 1) * half], ka))
        return jnp.concatenate(parts, axis=0)

    def update(s_list, v_list):
        m = m_scr[...]
        m_new = m
        for s in s_list:
            m_new = jnp.maximum(m_new, jnp.max(s, axis=-1, keepdims=True))
        alpha = jnp.exp2(m - m_new)
        l_new = alpha * l_scr[...]
        a_new = alpha * acc[...]
        for s, vview in zip(s_list, v_list):
            p = jnp.exp2(s - m_new)
            l_new = l_new + jnp.sum(p, axis=-1, keepdims=True)
            a_new = a_new + _dot(p.astype(BF16), vview.astype(BF16))
        l_scr[...] = l_new
        acc[...] = a_new
        m_scr[...] = m_new

    def vrows(v_ref):
        v = v_ref[...]
        return v.reshape(v.shape[0] * v.shape[1], v.shape[2])

    update([scores(k_refs[pg]) + bias_ref[...] for pg in range(pages)],
           [vrows(v_refs[pg]) for pg in range(pages)])

    @pl.when(step == pl.num_programs(1) - 1)
    def _():
        update([scores(knew_ref) + nbias_ref[...]], [vnew_ref[...]])
        lam = _lambda(lam_ref, lam_init)
        o = acc[...] / l_scr[...]
        o = o[:half] - lam * o[half:]
        o_ref[...] = (_rms(o, sub_ref[...], SUBLN_EPS) * (1.0 - lam_init)).astype(BF16)


def _attn_sample_call(page_table, q_rows, cache_k, cache_v, k_new, v_new, lam_p, subln, *, lam_init):
    nseq, n_pages = page_table.shape
    n_pool, page, nk, hd = cache_k.shape
    _, _, nv, vd = cache_v.shape
    nt = k_new.shape[1]
    cache_k = cache_k.reshape(n_pool, page * nk, hd)
    k_new = k_new.reshape(nseq, nt * nk, hd)
    v_new = v_new.reshape(nseq, nt * nv, vd)
    pages = PAGES_PER_STEP
    nrow = q_rows.shape[1]
    row_head = (jnp.arange(nrow) // nt) % nv
    row_tok = jnp.arange(nrow) % nt
    lane_head = jnp.arange(page * nv) % nv
    bias = jnp.where(row_head[:, None] == lane_head[None, :], 0.0, NEG).astype(F32)
    nl_head, nl_tok = jnp.arange(nt * nv) % nv, jnp.arange(nt * nv) // nv
    nbias = jnp.where((row_head[:, None] == nl_head[None, :]) & (nl_tok[None, :] <= row_tok[:, None]),
                      0.0, NEG).astype(F32)
    body = functools.partial(_attn_sample_body, pages=pages, lam_init=lam_init)

    def page_spec(shape, pg):
        return pl.BlockSpec((None,) + shape,
                            lambda b, s, pt: (pt[b, s * pages + pg],) + (0,) * len(shape))

    def seq_spec(shape):
        return pl.BlockSpec((None,) + shape, lambda b, s, pt: (b,) + (0,) * len(shape))

    def const(shape):
        return pl.BlockSpec(shape, lambda b, s, pt: (0,) * len(shape))

    in_specs = ([seq_spec(q_rows.shape[1:]), const(bias.shape), seq_spec(k_new.shape[1:]),
                 seq_spec(v_new.shape[1:]), const(nbias.shape), const(lam_p.shape), const(subln.shape)]
                + [page_spec((page * nk, hd), pg) for pg in range(pages)]
                + [page_spec((page, nv, vd), pg) for pg in range(pages)])
    grid_spec = pltpu.PrefetchScalarGridSpec(
        num_scalar_prefetch=1, grid=(nseq, n_pages // pages), in_specs=in_specs,
        out_specs=seq_spec((nrow // 2, vd)),
        scratch_shapes=[pltpu.VMEM((nrow, 1), F32), pltpu.VMEM((nrow, 1), F32), pltpu.VMEM((nrow, vd), F32)])
    return pl.pallas_call(
        body, grid_spec=grid_spec, out_shape=jax.ShapeDtypeStruct((nseq, nrow // 2, vd), BF16),
        compiler_params=_cparams(("parallel", "arbitrary")), name="attn_sample")(
            page_table, q_rows, bias, k_new, v_new, nbias, lam_p, subln,
            *([cache_k] * pages), *([cache_v] * pages))


def _lambda_init(layer):
    return 0.8 - 0.6 * math.exp(-0.3 * layer)


def kernel(x_prompt, x_sample, cache_k, cache_v, state_ssm_re, state_ssm_im, page_table, norm_gains,
           ffn_w_gate, ffn_w_up, ffn_w_down, ssm_a_re, ssm_a_im, ssm_log_dt, ssm_b_re, ssm_b_im,
           ssm_c_re, ssm_c_im, ssm_d, ssm_w_glu_a, ssm_w_glu_b, kv_norm, w_k, w_v, attn_w_q,
           attn_lambda, attn_subln, attn_w_o):
    bsz, seq, d = x_prompt.shape
    nseq, nt, _ = x_sample.shape
    n_kh, hd = cache_k.shape[2:]
    n_vh, vd = cache_v.shape[2:]
    n_groups, n_state = ssm_a_re.shape[1:]
    nb = d // LANES
    sw = GROUPS_PER_BLOCK * n_state

    gains = norm_gains[:, :, None, :]
    wg, wu, wd = ffn_w_gate.astype(BF16), ffn_w_up.astype(BF16), ffn_w_down.astype(BF16)
    wa, wb = ssm_w_glu_a[0].astype(BF16), ssm_w_glu_b[0].astype(BF16)
    wk, wv = w_k.astype(BF16), w_v.astype(BF16)
    wq, wo = attn_w_q[0].astype(BF16), attn_w_o[0].astype(BF16)
    kvn = kv_norm[None, :]
    dsk = ssm_d[0][None, :]
    lam_p, subln = attn_lambda[0], attn_subln[0][None, :]
    lam_init = _lambda_init(1)

    lb_re, lb_im, f_re, f_im = _s5_discretise(ssm_a_re[0], ssm_a_im[0], ssm_log_dt[0])
    mw, v_op, tab, b_tok, c_tok, lam_tok = _s5_operators(
        lb_re, lb_im, f_re, f_im, ssm_b_re[0], ssm_b_im[0], ssm_c_re[0], ssm_c_im[0], CHUNK)

    def ffn(x, layer, idx, tm, **kw):
        return _ffn_call(x, gains[layer, 4 * idx], gains[layer, 4 * idx + 1],
                         wg[layer, idx], wu[layer, idx], wd[layer, idx], tm=tm, **kw)

    def blocks(h):
        return jnp.moveaxis(h.reshape(h.shape[:-2] + (nb, sw)), -2, 0)

    def unblocks(h):
        return jnp.moveaxis(h, 0, -2).reshape(h.shape[1:-1] + (n_groups, n_state))

    def groups(h):
        return h.reshape(h.shape[:-2] + (n_groups, n_state))

    q_scale = hd ** -0.5 * math.log2(math.e)
    kv_args = dict(kvn=kvn, wk=wk, wv=wv, v_heads=n_vh)
    q_args = dict(g_q=gains[1, 2], wq=wq, q_scale=q_scale)

    def k_rows(r, lead):
        return r.reshape(lead + (n_kh, hd))

    def v_rows(r, lead):
        tiles = vd // LANES
        return r.reshape(lead + (tiles, n_vh, LANES)).swapaxes(-3, -2).reshape(lead + (n_vh, vd))

    tm_p = min(FFN_ROWS, bsz * seq)
    xp = x_prompt.reshape(bsz * seq, d)
    (x1,) = ffn(xp, 0, 0, tm_p)
    x2, sre_p, sim_p = _s5_prompt_call(x1.reshape(bsz, seq, d), gains[0, 2], gains[0, 3], dsk,
                                       mw, v_op, tab, wa, wb, tb=min(S5_ROWS, seq))
    x3, k32_p, v32_p, k16_p, v16_p = ffn(x2.reshape(bsz * seq, d), 0, 1, tm_p, **kv_args)
    x4, q_p = ffn(x3, 1, 0, tm_p, **q_args)
    o_p = _attn_prompt_call(q_p, k16_p, v16_p, lam_p, subln, bsz=bsz, seq=seq, n_heads=n_vh,
                            tq=min(ATTN_ROWS, seq), lam_init=lam_init)
    (y_p,) = ffn(x4, 1, 1, tm_p, o=o_p, wo=wo, g_o=gains[1, 3])

    tm_s = nseq * nt
    xs = x_sample.reshape(nseq * nt, d)
    (s1,) = ffn(xs, 0, 0, tm_s)
    s2, sre_s, sim_s = _s5_sample_call(s1.reshape(nseq, nt, d).transpose(1, 0, 2),
                                       blocks(state_ssm_re[0]), blocks(state_ssm_im[0]),
                                       gains[0, 2], gains[0, 3], dsk, b_tok, c_tok, lam_tok, wa, wb)
    s3, k32_s, v32_s, _, _ = ffn(s2.transpose(1, 0, 2).reshape(nseq * nt, d), 0, 1, tm_s, **kv_args)
    s4, q_s = ffn(s3, 1, 0, tm_s, **q_args)
    q_rows = q_s.reshape(nseq, nt, n_vh, 2, hd).transpose(0, 3, 2, 1, 4).reshape(nseq, 2 * n_vh * nt, hd)
    k_new = k_rows(k32_s, (nseq, nt))
    v_new = v_rows(v32_s, (nseq, nt))
    o_s = _attn_sample_call(page_table, q_rows, cache_k, cache_v, k_new, v_new, lam_p, subln,
                            lam_init=lam_init)
    o_s = o_s.reshape(nseq, n_vh, nt, vd).transpose(0, 2, 1, 3).reshape(nseq * nt, d)
    (y_s,) = ffn(s4, 1, 1, tm_s, o=o_s, wo=wo, g_o=gains[1, 3])

    return (y_p.reshape(bsz, seq, d), y_s.reshape(nseq, nt, d),
            k_rows(k32_p, (bsz, seq)), v_rows(v32_p, (bsz, seq)),
            groups(sre_p)[None], groups(sim_p)[None],
            k_new, v_new,
            unblocks(sre_s)[None], unblocks(sim_s)[None])
```

```python
import functools
import math

import jax
import jax.numpy as jnp
from jax import lax
from jax.experimental import pallas as pl
from jax.experimental.pallas import tpu as pltpu

F32 = jnp.float32
BF16 = jnp.bfloat16

EPS = 1e-6
SUBLN_EPS = 1e-5
NEG = -1e30

LANES = 128
SUBLANES = 8
VMEM_LIMIT = 56 * 1024 * 1024

SSM_GROUP = 16
SSM_STATE = 64
GROUPS_PER_BLOCK = LANES // SSM_GROUP
CHUNK = 4
PAGES_PER_STEP = 16
FFN_ROWS = 512
S5_ROWS = 512
ATTN_ROWS = 1024


def _cparams(sem):
    return pltpu.CompilerParams(dimension_semantics=sem, vmem_limit_bytes=VMEM_LIMIT)


def _const_spec(shape):
    nd = len(shape)
    return pl.BlockSpec(shape, lambda *_: (0,) * nd, pipeline_mode=pl.Buffered(1))


def _rms(x, g, eps):
    return x * lax.rsqrt(jnp.mean(x * x, axis=-1, keepdims=True) + eps) * g


def _dot(a, b):
    return jnp.dot(a, b, preferred_element_type=F32)


def _dot_nt(a, b):
    return lax.dot_general(a, b, (((1,), (1,)), ((), ())), preferred_element_type=F32)


def _ffn_body(*refs, has_o, has_kv, has_q, d_ff, f_chunk, v_heads, q_scale):
    it = iter(refs)
    x_ref = next(it)
    if has_o:
        o_ref, wo_ref, g3_ref = next(it), next(it), next(it)
    g0_ref, g1_ref, wg_ref, wu_ref, wd_ref = (next(it) for _ in range(5))
    if has_kv:
        kvn_ref, wk_ref, wv_ref = next(it), next(it), next(it)
    if has_q:
        g2_ref, wq_ref = next(it), next(it)
    y_ref = next(it)
    if has_kv:
        k32_ref, v32_ref, k16_ref, v16_ref = (next(it) for _ in range(4))
    if has_q:
        q_ref = next(it)

    x = x_ref[...]
    if has_o:
        x = x + _rms(_dot(o_ref[...], wo_ref[...]), g3_ref[...], EPS)
    h = _rms(x, g0_ref[...], EPS).astype(BF16)
    acc = None
    for c in range(0, d_ff, f_chunk):
        g = _dot(h, wg_ref[:, c:c + f_chunk])
        u = _dot(h, wu_ref[:, c:c + f_chunk])
        a = (g * jax.nn.sigmoid(g) * u).astype(BF16)
        d = _dot(a, wd_ref[c:c + f_chunk, :])
        acc = d if acc is None else acc + d
    y = x + 0.5 * _rms(acc, g1_ref[...], EPS)
    y_ref[...] = y
    if has_kv:
        kv_in = _rms(y, kvn_ref[...], EPS).astype(BF16)
        k = _dot(kv_in, wk_ref[...])
        v = _dot(kv_in, wv_ref[...])
        tm, d = k.shape
        rows = d // LANES
        for j in range(rows):
            k32_ref[pl.ds(j, tm, stride=rows), :] = k[:, j * LANES:(j + 1) * LANES]
        tiles = rows // v_heads
        for e in range(tiles):
            for hh in range(v_heads):
                c0 = (hh * tiles + e) * LANES
                v32_ref[pl.ds(e * v_heads + hh, tm, stride=rows), :] = v[:, c0:c0 + LANES]
        k16_ref[...] = k.astype(BF16)
        v16_ref[...] = v.astype(BF16)
    if has_q:
        hq = _rms(y, g2_ref[...], EPS).astype(BF16)
        q_ref[...] = (_dot(hq, wq_ref[...]) * q_scale).astype(BF16)


def _ffn_call(x, g_pre, g_post, wg, wu, wd, *, o=None, wo=None, g_o=None,
              kvn=None, wk=None, wv=None, v_heads=None, g_q=None, wq=None, q_scale=None, tm):
    n, d = x.shape
    rows = d // LANES
    d_ff = wg.shape[1]
    has_o, has_kv, has_q = o is not None, wk is not None, wq is not None
    row = pl.BlockSpec((tm, d), lambda i: (i, 0))
    args, specs = [x], [row]
    if has_o:
        args += [o, wo, g_o]
        specs += [row, _const_spec(wo.shape), _const_spec(g_o.shape)]
    args += [g_pre, g_post, wg, wu, wd]
    specs += [_const_spec(a.shape) for a in (g_pre, g_post, wg, wu, wd)]
    out_shape = [jax.ShapeDtypeStruct((n, d), F32)]
    out_specs = [row]
    if has_kv:
        args += [kvn, wk, wv]
        specs += [_const_spec(a.shape) for a in (kvn, wk, wv)]
        page_rows = pl.BlockSpec((tm * rows, LANES), lambda i: (i, 0))
        out_shape += [jax.ShapeDtypeStruct((n * rows, LANES), F32), jax.ShapeDtypeStruct((n * rows, LANES), F32),
                      jax.ShapeDtypeStruct((n, d), BF16), jax.ShapeDtypeStruct((n, d), BF16)]
        out_specs += [page_rows, page_rows, row, row]
    if has_q:
        args += [g_q, wq]
        specs += [_const_spec(g_q.shape), _const_spec(wq.shape)]
        out_shape.append(jax.ShapeDtypeStruct((n, d), BF16))
        out_specs.append(row)
    f_chunk = d_ff // 2 if (d_ff // 2) % LANES == 0 else d_ff
    body = functools.partial(_ffn_body, has_o=has_o, has_kv=has_kv, has_q=has_q,
                             d_ff=d_ff, f_chunk=f_chunk, v_heads=v_heads, q_scale=q_scale)
    return pl.pallas_call(
        body, grid=(n // tm,), in_specs=specs, out_specs=out_specs, out_shape=out_shape,
        compiler_params=_cparams(("parallel",)), name="ffn")(*args)


def _cmul(ar, ai, br, bi):
    return ar * br - ai * bi, ar * bi + ai * br


def _s5_disc_body(are_ref, aim_ref, ldt_ref, pwr_ref, pwi_ref, fr_ref, fi_ref):
    a_re, a_im = are_ref[...], aim_ref[...]
    dt = jnp.exp(ldt_ref[...])
    mag = jnp.exp(a_re * dt)
    lb_re = mag * jnp.cos(a_im * dt)
    lb_im = mag * jnp.sin(a_im * dt)
    num_re = lb_re - 1.0
    den = a_re * a_re + a_im * a_im
    fr_ref[...] = (num_re * a_re + lb_im * a_im) / den
    fi_ref[...] = (lb_im * a_re - num_re * a_im) / den
    p_re, p_im = jnp.ones_like(lb_re), jnp.zeros_like(lb_re)
    pwr_ref[0], pwi_ref[0] = p_re, p_im
    for j in range(1, pwr_ref.shape[0]):
        p_re, p_im = _cmul(p_re, p_im, lb_re, lb_im)
        pwr_ref[j], pwi_ref[j] = p_re, p_im


def _s5_discretise(a_re, a_im, log_dt, n_pow):
    g, n = a_re.shape
    out = jax.ShapeDtypeStruct((g, n), F32)
    pw = jax.ShapeDtypeStruct((n_pow + 1, g, n), F32)
    return pl.pallas_call(_s5_disc_body, out_shape=[pw, pw, out, out], name="s5_disc")(
        a_re, a_im, jnp.broadcast_to(log_dt[:, None], (g, n)))


def _s5_operators(pw_re, pw_im, f_re, f_im, b_re, b_im, c_re, c_im, chunk):
    _, g, n = pw_re.shape
    lb_re, lb_im = pw_re[1], pw_im[1]
    nb, gl = g // GROUPS_PER_BLOCK, GROUPS_PER_BLOCK
    c = b_re.shape[-1]
    bb_re = f_re[..., None] * b_re - f_im[..., None] * b_im
    bb_im = f_re[..., None] * b_im + f_im[..., None] * b_re
    def block_diag(x, perm, group_axis, n_col_axes):
        y = x.transpose(perm)
        split = y.ndim - n_col_axes
        n_rows, n_y = math.prod(y.shape[1:split]), y.shape[-1]
        n_x = math.prod(y.shape[split:-1])
        base = y.reshape(nb, n_rows, n_x * n_y)
        src, dst = jnp.arange(n_x * n_y), jnp.arange(n_x * gl * n_y)
        place = ((src[:, None] // n_y == dst[None, :] // (gl * n_y))
                 & (src[:, None] % n_y == dst[None, :] % n_y)).astype(F32)
        row_group = (jnp.arange(n_rows) // math.prod(y.shape[group_axis + 1:split])) % gl
        keep = (row_group[:, None] == (dst[None, :] // n_y) % gl).astype(F32)
        return jnp.einsum("brc,cd->brd", base, place, precision=lax.Precision.HIGHEST) * keep

    taps = []
    for j in range(chunk):
        dr, di = _cmul(c_re, c_im, pw_re[j][:, None, :], pw_im[j][:, None, :])
        tap = jnp.sum(dr[:, :, :, None] * bb_re[:, None, :, :] - di[:, :, :, None] * bb_im[:, None, :, :],
                      axis=2)
        taps.append(tap.swapaxes(1, 2))
    zero = jnp.zeros_like(taps[0])
    conv = jnp.stack([jnp.stack([taps[t - s] if t >= s else zero for t in range(chunk)], 0)
                      for s in range(chunk)], 0)
    conv = conv.reshape(chunk, chunk, nb, gl, c, c)
    m_op = block_diag(conv, (2, 0, 3, 4, 1, 5), 2, 2)

    w_parts = []
    for s in range(chunk):
        wr, wi = _cmul(pw_re[chunk - 1 - s][..., None], pw_im[chunk - 1 - s][..., None], bb_re, bb_im)
        w_parts.append(jnp.stack([wr, wi], 0))
    w = jnp.stack(w_parts, 0).reshape(chunk, 2, nb, gl, n, c)
    w_op = block_diag(w, (2, 0, 3, 5, 1, 4), 2, 2)

    v_parts = []
    for t in range(chunk):
        dr, di = _cmul(c_re, c_im, pw_re[t + 1][:, None, :], pw_im[t + 1][:, None, :])
        v_parts.append(jnp.stack([dr, -di], 0))
    v = jnp.stack(v_parts, 0).reshape(chunk, 2, nb, gl, c, n)
    v_op = block_diag(v, (2, 1, 3, 5, 0, 4), 2, 2)

    def lane(x):
        return x.reshape(nb, gl * n)
    rows = jnp.arange(SUBLANES)[None, :, None]
    tabs = []
    for sh in (1, 2, 4):
        m = (rows >= sh).astype(F32)
        tabs += [lane(pw_re[sh * chunk])[:, None, :] * m, lane(pw_im[sh * chunk])[:, None, :] * m]
    tabs += [jnp.stack([lane(pw_re[(r + 1) * chunk]) for r in range(SUBLANES)], 1),
             jnp.stack([lane(pw_im[(r + 1) * chunk]) for r in range(SUBLANES)], 1)]
    tab = jnp.stack(tabs, 1)

    b_tok = jnp.stack([bb_re, bb_im], 0).reshape(2, nb, gl, n, c)
    b_tok = block_diag(b_tok, (1, 2, 4, 0, 3), 1, 2)
    c_tok = jnp.stack([c_re, -c_im], 0).reshape(2, nb, gl, c, n)
    c_tok = block_diag(c_tok, (1, 0, 2, 4, 3), 2, 1)
    lam_tok = jnp.stack([lane(lb_re), lane(lb_im)], 1)
    return (jnp.concatenate([m_op, w_op], -1).astype(BF16), v_op.astype(BF16), tab,
            b_tok.astype(BF16), c_tok.astype(BF16), lam_tok)


def _gelu_tanh(x):
    return 0.5 * x * (1.0 + jnp.tanh(math.sqrt(2.0 / math.pi) * (x + 0.044715 * (x * x * x))))


def _s5_prompt_body(x_ref, g2_ref, g3_ref, dsk_ref, mw_ref, v_ref, tab_ref, wa_ref, wb_ref,
                    y_ref, sre_ref, sim_ref,
                    uslab, yslab, xg, state, *, chunk):
    tb, d = x_ref.shape
    nb = d // LANES
    rb = tb // chunk
    cw = chunk * LANES
    sw = state.shape[1] // 2
    tblk = pl.program_id(1)

    @pl.when(tblk == 0)
    def _():
        state[...] = jnp.zeros_like(state)

    u = _rms(x_ref[...], g2_ref[...], EPS)
    for b in range(nb):
        uslab[b] = u[:, b * LANES:(b + 1) * LANES]
    for b in range(nb):
        for t in range(chunk):
            xg[b, :, t * LANES:(t + 1) * LANES] = uslab[b, pl.ds(t, rb, stride=chunk), :].astype(BF16)

    row = lax.broadcasted_iota(jnp.int32, (SUBLANES, sw), 0)

    last = SUBLANES - 1
    for b in range(nb):
        r = _dot(xg[b], mw_ref[b])
        tab = tab_ref[b]
        zp_re, zp_im = state[b:b + 1, :sw], state[b:b + 1, sw:]
        ep_re = ep_im = jnp.zeros((1, sw), F32)
        e_tiles = []
        for k in range(rb // SUBLANES):
            z_re = r[k * SUBLANES:(k + 1) * SUBLANES, cw:cw + sw]
            z_im = r[k * SUBLANES:(k + 1) * SUBLANES, cw + sw:]
            s_re = jnp.where(row == 0, zp_re, pltpu.roll(z_re, 1, 0))
            s_im = jnp.where(row == 0, zp_im, pltpu.roll(z_im, 1, 0))
            for lvl, sh in enumerate((1, 2, 4)):
                p_re, p_im = pltpu.roll(s_re, sh, 0), pltpu.roll(s_im, sh, 0)
                t_re, t_im = tab[2 * lvl], tab[2 * lvl + 1]
                s_re, s_im = (s_re + t_re * p_re - t_im * p_im, s_im + t_re * p_im + t_im * p_re)
            e_re = s_re + tab[6] * ep_re - tab[7] * ep_im
            e_im = s_im + tab[6] * ep_im + tab[7] * ep_re
            e_tiles.append(jnp.concatenate([e_re, e_im], axis=-1))
            zp_re, zp_im, ep_re, ep_im = z_re[last:, :], z_im[last:, :], e_re[last:, :], e_im[last:, :]
        a_re, a_im = tab[6, :1, :], tab[7, :1, :]
        state[b:b + 1, :sw] = a_re * ep_re - a_im * ep_im + zp_re
        state[b:b + 1, sw:] = a_re * ep_im + a_im * ep_re + zp_im

        y = r[:, :cw] + _dot(jnp.concatenate(e_tiles, axis=0).astype(BF16), v_ref[b])
        for t in range(chunk):
            yslab[b, pl.ds(t, rb, stride=chunk), :] = y[:, t * LANES:(t + 1) * LANES]

    ys = jnp.concatenate([yslab[b] for b in range(nb)], axis=-1)
    g = _gelu_tanh(ys + dsk_ref[...] * u).astype(BF16)
    o = _dot(g, wa_ref[...]) * jax.nn.sigmoid(_dot(g, wb_ref[...]))
    y_ref[...] = x_ref[...] + _rms(o, g3_ref[...], EPS)

    @pl.when(tblk == pl.num_programs(1) - 1)
    def _():
        sre_ref[...] = state[:, :sw]
        sim_ref[...] = state[:, sw:]


def _s5_prompt_call(x, g2, g3, dsk, mw, v_op, tab, wa, wb, *, tb):
    bsz, seq, d = x.shape
    nb = d // LANES
    chunk = CHUNK
    rb = tb // chunk
    sw = tab.shape[-1]
    body = functools.partial(_s5_prompt_body, chunk=chunk)
    xspec = pl.BlockSpec((None, tb, d), lambda b, t: (b, t, 0))
    sspec = pl.BlockSpec((None, nb, sw), lambda b, t: (b, 0, 0))
    consts = (g2, g3, dsk, mw, v_op, tab, wa, wb)
    return pl.pallas_call(
        body, grid=(bsz, seq // tb),
        in_specs=[xspec] + [_const_spec(a.shape) for a in consts],
        out_specs=[xspec, sspec, sspec],
        out_shape=[jax.ShapeDtypeStruct((bsz, seq, d), F32),
                   jax.ShapeDtypeStruct((bsz, nb, sw), F32), jax.ShapeDtypeStruct((bsz, nb, sw), F32)],
        scratch_shapes=[pltpu.VMEM((nb, tb, LANES), F32), pltpu.VMEM((nb, tb, LANES), F32),
                        pltpu.VMEM((nb, rb, chunk * LANES), BF16),
                        pltpu.VMEM((nb, 2 * sw), F32)],
        compiler_params=_cparams(("parallel", "arbitrary")), name="s5_prompt")(x, *consts)


def _s5_sample_body(x_ref, hre_ref, him_ref, g2_ref, g3_ref, dsk_ref, bt_ref, ct_ref, lam_ref,
                    wa_ref, wb_ref, y_ref, sre_ref, sim_ref, yscr):
    nt, nseq, d = x_ref.shape
    nb = d // LANES
    sw = lam_ref.shape[-1]
    h_re = [hre_ref[b] for b in range(nb)]
    h_im = [him_ref[b] for b in range(nb)]
    for t in range(nt):
        u = _rms(x_ref[t], g2_ref[...], EPS)
        ub = u.astype(BF16)
        for b in range(nb):
            bu = _dot(ub[:, b * LANES:(b + 1) * LANES], bt_ref[b])
            l_re, l_im = lam_ref[b, 0:1, :], lam_ref[b, 1:2, :]
            n_re = l_re * h_re[b] - l_im * h_im[b] + bu[:, :sw]
            n_im = l_re * h_im[b] + l_im * h_re[b] + bu[:, sw:]
            h_re[b], h_im[b] = n_re, n_im
            hb = jnp.concatenate([n_re, n_im], axis=-1).astype(BF16)
            yscr[t, :, b * LANES:(b + 1) * LANES] = _dot(hb, ct_ref[b])
        yb = yscr[t] + dsk_ref[...] * u
        g = _gelu_tanh(yb).astype(BF16)
        o = _dot(g, wa_ref[...]) * jax.nn.sigmoid(_dot(g, wb_ref[...]))
        y_ref[t] = x_ref[t] + _rms(o, g3_ref[...], EPS)
    for b in range(nb):
        sre_ref[b] = h_re[b]
        sim_ref[b] = h_im[b]


def _s5_sample_call(x, h_re, h_im, g2, g3, dsk, b_tok, c_tok, lam_tok, wa, wb):
    nt, nseq, d = x.shape
    args = (x, h_re, h_im, g2, g3, dsk, b_tok, c_tok, lam_tok, wa, wb)
    return pl.pallas_call(
        _s5_sample_body,
        out_shape=[jax.ShapeDtypeStruct(x.shape, F32), jax.ShapeDtypeStruct(h_re.shape, F32),
                   jax.ShapeDtypeStruct(h_im.shape, F32)],
        scratch_shapes=[pltpu.VMEM((nt, nseq, d), F32)],
        compiler_params=pltpu.CompilerParams(vmem_limit_bytes=VMEM_LIMIT), name="s5_sample")(*args)


def _lambda(lam_ref, lam_init):
    lp = lam_ref[...]
    s1 = jnp.sum(lp[0:1] * lp[1:2], axis=-1, keepdims=True)
    s2 = jnp.sum(lp[2:3] * lp[3:4], axis=-1, keepdims=True)
    return jnp.exp(s1) - jnp.exp(s2) + lam_init


def _attn_prompt_body(q_ref, k_ref, v_ref, lam_ref, sub_ref, o_ref, s_scr, acc, *, lam_init):
    tq, hd2 = q_ref.shape
    hd = hd2 // 2
    hq = tq // 2
    i = pl.program_id(2)
    acc[...] = jnp.zeros_like(acc)

    def update(h, s, vb, m, l, rows):
        m_new = jnp.maximum(m, jnp.max(s, axis=-1, keepdims=True))
        alpha = jnp.exp2(m - m_new)
        p = jnp.exp2(s - m_new)
        l_new = alpha * l + jnp.sum(p, axis=-1, keepdims=True)
        acc[h, rows, :] = alpha * acc[h, rows, :] + _dot(p.astype(BF16), vb)
        return m_new, l_new

    def full_block(j, carry):
        r0 = pl.multiple_of(j * tq, tq)
        kb = k_ref[pl.ds(r0, tq), :]
        vb = v_ref[pl.ds(r0, tq), :]
        for h in range(2):
            s_scr[h] = _dot_nt(q_ref[:, h * hd:(h + 1) * hd], kb[:, h * hd:(h + 1) * hd])
        out = []
        for h in range(2):
            out += update(h, s_scr[h], vb, carry[2 * h], carry[2 * h + 1], slice(None))
        return tuple(out)

    init = (jnp.full((tq, 1), NEG, F32), jnp.zeros((tq, 1), F32)) * 2
    carry = lax.fori_loop(0, i, full_block, init)

    d0 = pl.multiple_of(i * tq, tq)
    out = []
    for h in range(2):
        m, l = carry[2 * h], carry[2 * h + 1]
        stats = []
        for r0, nk in ((0, hq), (hq, tq)):
            kb = k_ref[pl.ds(d0, nk), h * hd:(h + 1) * hd]
            vb = v_ref[pl.ds(d0, nk), :]
            s = _dot_nt(q_ref[r0:r0 + hq, h * hd:(h + 1) * hd], kb)
            row = lax.broadcasted_iota(jnp.int32, (hq, nk), 0) + r0
            col = lax.broadcasted_iota(jnp.int32, (hq, nk), 1)
            s = jnp.where(col <= row, s, NEG)
            stats.append(update(h, s, vb, m[r0:r0 + hq], l[r0:r0 + hq], slice(r0, r0 + hq)))
        out.append(jnp.concatenate([stats[0][1], stats[1][1]], axis=0))
    l1, l2 = out
    lam = _lambda(lam_ref, lam_init)
    o = acc[0] / l1 - lam * (acc[1] / l2)
    o_ref[...] = (_rms(o, sub_ref[...], SUBLN_EPS) * (1.0 - lam_init)).astype(BF16)


def _attn_prompt_call(q, k, v, lam_p, subln, *, bsz, seq, n_heads, tq, lam_init):
    n, d = q.shape
    hw = d // n_heads
    nq = seq // tq
    body = functools.partial(_attn_prompt_body, lam_init=lam_init)
    qspec = pl.BlockSpec((tq, hw), lambda b, h, i: (b * nq + i, h))
    kvspec = pl.BlockSpec((seq, hw), lambda b, h, i: (b, h))
    return pl.pallas_call(
        body, grid=(bsz, n_heads, nq),
        in_specs=[qspec, kvspec, kvspec, _const_spec(lam_p.shape), _const_spec(subln.shape)],
        out_specs=qspec, out_shape=jax.ShapeDtypeStruct((n, d), BF16),
        scratch_shapes=[pltpu.VMEM((2, tq, tq), F32), pltpu.VMEM((2, tq, hw), F32)],
        compiler_params=_cparams(("parallel", "parallel", "arbitrary")), name="attn_prompt")(
            q, k, v, lam_p, subln)


def _attn_sample_body(pt_ref, q_ref, bias_ref, knew_ref, vnew_ref, nbias_ref, lam_ref, sub_ref, *rest,
                      pages, lam_init):
    k_refs, v_refs = rest[:pages], rest[pages:2 * pages]
    o_ref, m_scr, l_scr, acc = rest[2 * pages:]
    step = pl.program_id(1)
    nrow = q_ref.shape[0]
    half = nrow // 2

    @pl.when(step == 0)
    def _():
        m_scr[...] = jnp.full_like(m_scr, NEG)
        l_scr[...] = jnp.zeros_like(l_scr)
        acc[...] = jnp.zeros_like(acc)

    q = q_ref[...]

    def scores(kview):
        rows = kview.shape[0] // 2
        parts = []
        for a in range(2):
            ka = kview[pl.ds(a, rows, stride=2), :].astype(BF16)
            parts.append(_dot_nt(q[a * half:(a + 1) * half], ka))
        return jnp.concatenate(parts, axis=0)

    def update(s_list, v_list):
        m = m_scr[...]
        m_new = m
        for s in s_list:
            m_new = jnp.maximum(m_new, jnp.max(s, axis=-1, keepdims=True))
        alpha = jnp.exp2(m - m_new)
        l_new = alpha * l_scr[...]
        a_new = alpha * acc[...]
        for s, vview in zip(s_list, v_list):
            p = jnp.exp2(s - m_new)
            l_new = l_new + jnp.sum(p, axis=-1, keepdims=True)
            a_new = a_new + _dot(p.astype(BF16), vview.astype(BF16))
        l_scr[...] = l_new
        acc[...] = a_new
        m_scr[...] = m_new

    def vrows(v_ref):
        v = v_ref[...]
        return v.reshape(v.shape[0] * v.shape[1], v.shape[2])

    update([scores(k_refs[pg]) + bias_ref[...] for pg in range(pages)],
           [vrows(v_refs[pg]) for pg in range(pages)])

    @pl.when(step == pl.num_programs(1) - 1)
    def _():
        update([scores(knew_ref) + nbias_ref[...]], [vnew_ref[...]])
        lam = _lambda(lam_ref, lam_init)
        o = acc[...] / l_scr[...]
        o = o[:half] - lam * o[half:]
        o_ref[...] = (_rms(o, sub_ref[...], SUBLN_EPS) * (1.0 - lam_init)).astype(BF16)


def _attn_sample_call(page_table, q_rows, cache_k, cache_v, k_new, v_new, lam_p, subln, *, lam_init):
    nseq, n_pages = page_table.shape
    n_pool, page, nk, hd = cache_k.shape
    _, _, nv, vd = cache_v.shape
    nt = k_new.shape[1]
    cache_k = cache_k.reshape(n_pool, page * nk, hd)
    k_new = k_new.reshape(nseq, nt * nk, hd)
    v_new = v_new.reshape(nseq, nt * nv, vd)
    pages = PAGES_PER_STEP
    nrow = q_rows.shape[1]
    row_head = (jnp.arange(nrow) // nt) % nv
    row_tok = jnp.arange(nrow) % nt
    lane_head = jnp.arange(page * nv) % nv
    bias = jnp.where(row_head[:, None] == lane_head[None, :], 0.0, NEG).astype(F32)
    nl_head, nl_tok = jnp.arange(nt * nv) % nv, jnp.arange(nt * nv) // nv
    nbias = jnp.where((row_head[:, None] == nl_head[None, :]) & (nl_tok[None, :] <= row_tok[:, None]),
                      0.0, NEG).astype(F32)
    body = functools.partial(_attn_sample_body, pages=pages, lam_init=lam_init)

    def page_spec(shape, pg):
        return pl.BlockSpec((None,) + shape,
                            lambda b, s, pt: (pt[b, s * pages + pg],) + (0,) * len(shape))

    def seq_spec(shape):
        return pl.BlockSpec((None,) + shape, lambda b, s, pt: (b,) + (0,) * len(shape))

    def const(shape):
        return pl.BlockSpec(shape, lambda b, s, pt: (0,) * len(shape))

    in_specs = ([seq_spec(q_rows.shape[1:]), const(bias.shape), seq_spec(k_new.shape[1:]),
                 seq_spec(v_new.shape[1:]), const(nbias.shape), const(lam_p.shape), const(subln.shape)]
                + [page_spec((page * nk, hd), pg) for pg in range(pages)]
                + [page_spec((page, nv, vd), pg) for pg in range(pages)])
    grid_spec = pltpu.PrefetchScalarGridSpec(
        num_scalar_prefetch=1, grid=(nseq, n_pages // pages), in_specs=in_specs,
        out_specs=seq_spec((nrow // 2, vd)),
        scratch_shapes=[pltpu.VMEM((nrow, 1), F32), pltpu.VMEM((nrow, 1), F32), pltpu.VMEM((nrow, vd), F32)])
    return pl.pallas_call(
        body, grid_spec=grid_spec, out_shape=jax.ShapeDtypeStruct((nseq, nrow // 2, vd), BF16),
        compiler_params=_cparams(("parallel", "arbitrary")), name="attn_sample")(
            page_table, q_rows, bias, k_new, v_new, nbias, lam_p, subln,
            *([cache_k] * pages), *([cache_v] * pages))


def _lambda_init(layer):
    return 0.8 - 0.6 * math.exp(-0.3 * layer)


def kernel(x_prompt, x_sample, cache_k, cache_v, state_ssm_re, state_ssm_im, page_table, norm_gains,
           ffn_w_gate, ffn_w_up, ffn_w_down, ssm_a_re, ssm_a_im, ssm_log_dt, ssm_b_re, ssm_b_im,
           ssm_c_re, ssm_c_im, ssm_d, ssm_w_glu_a, ssm_w_glu_b, kv_norm, w_k, w_v, attn_w_q,
           attn_lambda, attn_subln, attn_w_o):
    bsz, seq, d = x_prompt.shape
    nseq, nt, _ = x_sample.shape
    n_kh, hd = cache_k.shape[2:]
    n_vh, vd = cache_v.shape[2:]
    n_groups, n_state = ssm_a_re.shape[1:]
    nb = d // LANES
    sw = GROUPS_PER_BLOCK * n_state

    gains = norm_gains[:, :, None, :]
    wg, wu, wd = ffn_w_gate.astype(BF16), ffn_w_up.astype(BF16), ffn_w_down.astype(BF16)
    wa, wb = ssm_w_glu_a[0].astype(BF16), ssm_w_glu_b[0].astype(BF16)
    wk, wv = w_k.astype(BF16), w_v.astype(BF16)
    wq, wo = attn_w_q[0].astype(BF16), attn_w_o[0].astype(BF16)
    kvn = kv_norm[None, :]
    dsk = ssm_d[0][None, :]
    lam_p, subln = attn_lambda[0], attn_subln[0][None, :]
    lam_init = _lambda_init(1)

    pw_re, pw_im, f_re, f_im = _s5_discretise(ssm_a_re[0], ssm_a_im[0], ssm_log_dt[0], SUBLANES * CHUNK)
    mw, v_op, tab, b_tok, c_tok, lam_tok = _s5_operators(
        pw_re, pw_im, f_re, f_im, ssm_b_re[0], ssm_b_im[0], ssm_c_re[0], ssm_c_im[0], CHUNK)

    def ffn(x, layer, idx, tm, **kw):
        return _ffn_call(x, gains[layer, 4 * idx], gains[layer, 4 * idx + 1],
                         wg[layer, idx], wu[layer, idx], wd[layer, idx], tm=tm, **kw)

    def blocks(h):
        return jnp.moveaxis(h.reshape(h.shape[:-2] + (nb, sw)), -2, 0)

    def unblocks(h):
        return jnp.moveaxis(h, 0, -2).reshape(h.shape[1:-1] + (n_groups, n_state))

    def groups(h):
        return h.reshape(h.shape[:-2] + (n_groups, n_state))

    q_scale = hd ** -0.5 * math.log2(math.e)
    kv_args = dict(kvn=kvn, wk=wk, wv=wv, v_heads=n_vh)
    q_args = dict(g_q=gains[1, 2], wq=wq, q_scale=q_scale)

    def k_rows(r, lead):
        return r.reshape(lead + (n_kh, hd))

    def v_rows(r, lead):
        tiles = vd // LANES
        return r.reshape(lead + (tiles, n_vh, LANES)).swapaxes(-3, -2).reshape(lead + (n_vh, vd))

    tm_p = min(FFN_ROWS, bsz * seq)
    xp = x_prompt.reshape(bsz * seq, d)
    (x1,) = ffn(xp, 0, 0, tm_p)
    x2, sre_p, sim_p = _s5_prompt_call(x1.reshape(bsz, seq, d), gains[0, 2], gains[0, 3], dsk,
                                       mw, v_op, tab, wa, wb, tb=min(S5_ROWS, seq))
    x3, k32_p, v32_p, k16_p, v16_p = ffn(x2.reshape(bsz * seq, d), 0, 1, tm_p, **kv_args)
    x4, q_p = ffn(x3, 1, 0, tm_p, **q_args)
    o_p = _attn_prompt_call(q_p, k16_p, v16_p, lam_p, subln, bsz=bsz, seq=seq, n_heads=n_vh,
                            tq=min(ATTN_ROWS, seq), lam_init=lam_init)
    (y_p,) = ffn(x4, 1, 1, tm_p, o=o_p, wo=wo, g_o=gains[1, 3])

    tm_s = nseq * nt
    xs = x_sample.reshape(nseq * nt, d)
    (s1,) = ffn(xs, 0, 0, tm_s)
    s2, sre_s, sim_s = _s5_sample_call(s1.reshape(nseq, nt, d).transpose(1, 0, 2),
                                       blocks(state_ssm_re[0]), blocks(state_ssm_im[0]),
                                       gains[0, 2], gains[0, 3], dsk, b_tok, c_tok, lam_tok, wa, wb)
    s3, k32_s, v32_s, _, _ = ffn(s2.transpose(1, 0, 2).reshape(nseq * nt, d), 0, 1, tm_s, **kv_args)
    s4, q_s = ffn(s3, 1, 0, tm_s, **q_args)
    q_rows = q_s.reshape(nseq, nt, n_vh, 2, hd).transpose(0, 3, 2, 1, 4).reshape(nseq, 2 * n_vh * nt, hd)
    k_new = k_rows(k32_s, (nseq, nt))
    v_new = v_rows(v32_s, (nseq, nt))
    o_s = _attn_sample_call(page_table, q_rows, cache_k, cache_v, k_new, v_new, lam_p, subln,
                            lam_init=lam_init)
    o_s = o_s.reshape(nseq, n_vh, nt, vd).transpose(0, 2, 1, 3).reshape(nseq * nt, d)
    (y_s,) = ffn(s4, 1, 1, tm_s, o=o_s, wo=wo, g_o=gains[1, 3])

    return (y_p.reshape(bsz, seq, d), y_s.reshape(nseq, nt, d),
            k_rows(k32_p, (bsz, seq)), v_rows(v32_p, (bsz, seq)),
            groups(sre_p)[None], groups(sim_p)[None],
            k_new, v_new,
            unblocks(sre_s)[None], unblocks(sim_s)[None])
```
